```python
import jax, jax.numpy as jnp
from jax import lax
import numpy as np

D_MODEL = 1024
BATCH = 8
SEQ = 2048
DEPTH = 4
DEC_BATCH = 128
DEC_SEQ = 1
PAST_LEN = 8192
PAGE_SIZE = 128

MLA_HEADS = 8
QK_NOPE = 64
QK_ROPE = 32
V_HEAD = 64
Q_RANK = 384
KV_RANK = 256
MLA_WIDTH = MLA_HEADS * V_HEAD
MLA_PROJ = Q_RANK + KV_RANK + QK_ROPE
ATTN_SCALE = (QK_NOPE + QK_ROPE) ** -0.5
ROPE_BASE = 10000.0
RWKV_HEADS = 8
RWKV_HEAD = 64
RWKV_WIDTH = RWKV_HEADS * RWKV_HEAD
DECAY_LORA = 64
AAA_LORA = 64
GATE_LORA = 128
RWKV_PROJ = 3 * RWKV_WIDTH + DECAY_LORA + AAA_LORA + GATE_LORA
LNX_EPS = 64e-5
N_BRANCH = 2
P_IN = MLA_PROJ + RWKV_PROJ + N_BRANCH * D_MODEL
D_FF = 4 * D_MODEL
NORM_EPS = 1e-6
Q_BLOCK = 128

kernel_name = "mla_rwkv7_gated_sandwich_adaln_decoder_step"


def rms_norm(x, g):
    xf = x.astype(jnp.float32)
    y = xf * lax.rsqrt(jnp.mean(xf * xf, axis=-1, keepdims=True) + NORM_EPS)
    return (y * g.astype(jnp.float32)).astype(x.dtype)


def rope(x, pos):
    half = QK_ROPE // 2
    inv = ROPE_BASE ** (-jnp.arange(half, dtype=jnp.float32) / half)
    ang = pos.astype(jnp.float32)[:, None] * inv[None, :]
    ang = ang.reshape((ang.shape[0],) + (1,) * (x.ndim - 3) + (half,))
    cos, sin = jnp.cos(ang), jnp.sin(ang)
    x1 = x[..., :half].astype(jnp.float32)
    x2 = x[..., half:].astype(jnp.float32)
    return jnp.concatenate([x1 * cos - x2 * sin, x1 * sin + x2 * cos], axis=-1).astype(x.dtype)


def modulation(c, w_ada, b_ada):
    m = jax.nn.silu(c) @ w_ada + b_ada
    return [t[:, None, :] for t in jnp.split(m, 6, axis=-1)]


def mla_project(z_mla, pos, g_q_norm, w_uq, g_kv_norm):
    b, s = z_mla.shape[:2]
    q_lat, c_kv, k_pe = jnp.split(z_mla, [Q_RANK, Q_RANK + KV_RANK], axis=-1)
    q = (rms_norm(q_lat, g_q_norm) @ w_uq).reshape(b, s, MLA_HEADS, QK_NOPE + QK_ROPE)
    q_nope, q_pe = q[..., :QK_NOPE], rope(q[..., QK_NOPE:], pos)
    return q_nope, q_pe, rms_norm(c_kv, g_kv_norm), rope(k_pe, pos)


def mla_prompt_attention(q_nope, q_pe, c_kv, k_pe, w_ukv):
    b, s = q_nope.shape[:2]
    kv = jnp.einsum('bsr,rhe->bshe', c_kv, w_ukv.reshape(KV_RANK, MLA_HEADS, QK_NOPE + V_HEAD))
    k_nope, v = kv[..., :QK_NOPE], kv[..., QK_NOPE:]
    nb = s // Q_BLOCK
    qn_b = q_nope.reshape(b, nb, Q_BLOCK, MLA_HEADS, QK_NOPE).transpose(1, 0, 2, 3, 4)
    qp_b = q_pe.reshape(b, nb, Q_BLOCK, MLA_HEADS, QK_ROPE).transpose(1, 0, 2, 3, 4)
    key_pos = jnp.arange(s)

    def block(args):
        i, qn, qp = args
        sc = jnp.einsum('bqhd,bkhd->bhqk', qn, k_nope) + jnp.einsum('bqhd,bkd->bhqk', qp, k_pe)
        sc = sc.astype(jnp.float32) * ATTN_SCALE
        q_pos = i * Q_BLOCK + jnp.arange(Q_BLOCK)
        sc = jnp.where(key_pos[None, :] <= q_pos[:, None], sc, -jnp.inf)
        p = jax.nn.softmax(sc, axis=-1).astype(v.dtype)
        return jnp.einsum('bhqk,bkhd->bqhd', p, v)

    o = lax.map(block, (jnp.arange(nb), qn_b, qp_b))
    return o.transpose(1, 0, 2, 3, 4).reshape(b, s, MLA_WIDTH)


def mla_sample_attention(q_nope, q_pe, c_kv, k_pe, cache_ckv, cache_kpe, l, page_table, w_ukv):
    b, n_new = q_nope.shape[:2]
    past_len = page_table.shape[1] * cache_ckv.shape[2]
    past_ckv = cache_ckv[l, page_table].reshape(b, past_len, KV_RANK)
    past_kpe = cache_kpe[l, page_table].reshape(b, past_len, QK_ROPE)
    w = w_ukv.reshape(KV_RANK, MLA_HEADS, QK_NOPE + V_HEAD)
    w_uk, w_uv = w[..., :QK_NOPE], w[..., QK_NOPE:]
    q_lat = jnp.einsum('bqhd,rhd->bqhr', q_nope, w_uk)
    s_past = jnp.einsum('bqhr,bkr->bhqk', q_lat, past_ckv) + jnp.einsum('bqhd,bkd->bhqk', q_pe, past_kpe)
    s_cur = jnp.einsum('bqhr,bkr->bhqk', q_lat, c_kv) + jnp.einsum('bqhd,bkd->bhqk', q_pe, k_pe)
    s_cur = s_cur.astype(jnp.float32) * ATTN_SCALE
    causal = jnp.arange(n_new)[None, :] <= jnp.arange(n_new)[:, None]
    s_cur = jnp.where(causal, s_cur, -jnp.inf)
    sc = jnp.concatenate([s_past.astype(jnp.float32) * ATTN_SCALE, s_cur], axis=-1)
    p = jax.nn.softmax(sc, axis=-1).astype(c_kv.dtype)
    o_lat = (jnp.einsum('bhqk,bkr->bqhr', p[..., :past_len], past_ckv)
             + jnp.einsum('bhqk,bkr->bqhr', p[..., past_len:], c_kv))
    o = jnp.einsum('bqhr,rhd->bqhd', o_lat, w_uv)
    return o.reshape(b, n_new, MLA_WIDTH)


def wkv7_scan(r, w, k, v, a, bb, state0):
    def step(st, inp):
        r_t, w_t, k_t, v_t, a_t, b_t = inp
        sa = jnp.einsum('bhvk,bhk->bhv', st, a_t)
        st = st * w_t[:, :, None, :] + sa[..., None] * b_t[:, :, None, :] + v_t[..., None] * k_t[:, :, None, :]
        return st, jnp.einsum('bhvk,bhk->bhv', st, r_t)
    xs = tuple(t.astype(jnp.float32).swapaxes(0, 1) for t in (r, w, k, v, a, bb))
    st, ys = lax.scan(step, state0.astype(jnp.float32), xs)
    return ys.swapaxes(0, 1), st


def rwkv7_time_mix(z_mix, state0, w0, w2, a0, a2, g2, k_k, k_a, r_k, lnx_w, lnx_b):
    b, s = z_mix.shape[:2]
    hs = (b, s, RWKV_HEADS, RWKV_HEAD)
    splits = [RWKV_WIDTH, 2 * RWKV_WIDTH, 3 * RWKV_WIDTH,
              3 * RWKV_WIDTH + DECAY_LORA, 3 * RWKV_WIDTH + DECAY_LORA + AAA_LORA]
    r, k, v, xw, xa, xg = jnp.split(z_mix, splits, axis=-1)
    w_log = -jax.nn.softplus(-(w0 + jnp.tanh(xw) @ w2)) - 0.5
    decay = jnp.exp(-jnp.exp(w_log.astype(jnp.float32)))
    a = jax.nn.sigmoid(a0 + xa @ a2)
    g = jax.nn.sigmoid(xg) @ g2
    kk = (k * k_k).reshape(hs).astype(jnp.float32)
    kk = kk / jnp.maximum(jnp.sqrt(jnp.sum(kk * kk, axis=-1, keepdims=True)), 1e-12)
    k = k * (1 + (a - 1) * k_a)
    r_h, k_h, v_h, a_h = (t.reshape(hs) for t in (r, k, v, a))
    out, st = wkv7_scan(r_h, decay.reshape(hs), k_h, v_h, -kk, kk * a_h, state0)
    mean = jnp.mean(out, axis=-1, keepdims=True)
    var = jnp.mean(jnp.square(out - mean), axis=-1, keepdims=True)
    out = ((out - mean) * lax.rsqrt(var + LNX_EPS)).reshape(b, s, RWKV_WIDTH).astype(z_mix.dtype)
    out = out * lnx_w + lnx_b
    bonus = (jnp.sum(r_h * k_h * r_k, axis=-1, keepdims=True) * v_h).reshape(b, s, RWKV_WIDTH)
    return (out + bonus) * g, st.astype(state0.dtype)


def run_trunk(x, c, pos, shift0, wkv0, attend,
              w_ada, b_ada, g_pre_mix, g_post_mix, g_pre_ff, g_post_ff, w_in, mu_shift,
              g_q_norm, w_uq, g_kv_norm, w_ukv, rw_w0, rw_w2, rw_a0, rw_a2, rw_g2, rw_k_k,
              rw_k_a, rw_r_k, rw_lnx_w, rw_lnx_b, w_branch_mla, w_branch_rwkv, w_out, w_ff1, w_ff2):
    ckv_rows, kpe_rows, wkv_states, shift_rows = [], [], [], []
    for l in range(DEPTH):
        sh1, sc1, gt1, sh2, sc2, gt2 = modulation(c, w_ada[l], b_ada[l])
        h = rms_norm(x, g_pre_mix[l]) * (1 + sc1) + sh1
        z = h @ w_in[l]
        z_mla, z_rw, z_gate = jnp.split(z, [MLA_PROJ, MLA_PROJ + RWKV_PROJ], axis=-1)
        q_nope, q_pe, c_kv, k_pe = mla_project(z_mla, pos, g_q_norm[l], w_uq[l], g_kv_norm[l])
        y_mla = attend(l, q_nope, q_pe, c_kv, k_pe, w_ukv[l])
        z_prev = jnp.concatenate([shift0[l][:, None, :], z_rw[:, :-1]], axis=1)
        z_mix = z_rw + mu_shift[l] * (z_prev - z_rw)
        y_rw, wkv_new = rwkv7_time_mix(z_mix, wkv0[l], rw_w0[l], rw_w2[l], rw_a0[l], rw_a2[l],
                                       rw_g2[l], rw_k_k[l], rw_k_a[l], rw_r_k[l],
                                       rw_lnx_w[l], rw_lnx_b[l])
        g_mla, g_rw = jnp.split(jax.nn.sigmoid(z_gate), N_BRANCH, axis=-1)
        merged = g_mla * (y_mla @ w_branch_mla[l]) + g_rw * (y_rw @ w_branch_rwkv[l])
        x = x + gt1 * rms_norm(merged @ w_out[l], g_post_mix[l])
        h = rms_norm(x, g_pre_ff[l]) * (1 + sc2) + sh2
        f = jnp.square(jax.nn.relu(h @ w_ff1[l])) @ w_ff2[l]
        x = x + gt2 * rms_norm(f, g_post_ff[l])
        ckv_rows.append(c_kv)
        kpe_rows.append(k_pe)
        wkv_states.append(wkv_new)
        shift_rows.append(z_rw[:, -1])
    return x, jnp.stack(ckv_rows), jnp.stack(kpe_rows), jnp.stack(wkv_states), jnp.stack(shift_rows)


def setup_inputs(seed: int = 0) -> dict:
    key = jax.random.key(seed)
    ks = iter(jax.random.split(key, 64))
    f32 = jnp.float32

    def nrm(shape, scale):
        return jax.random.normal(next(ks), shape, f32) * scale

    def unif(shape, lo, hi):
        return jax.random.uniform(next(ks), shape, f32, lo, hi)

    def gain(shape):
        return 1.0 + nrm(shape, 0.05)

    n_pages = PAST_LEN // PAGE_SIZE
    n_pool = (DEC_BATCH * n_pages * 5) // 4
    page_table = jax.random.permutation(next(ks), n_pool)[:DEC_BATCH * n_pages]
    page_table = page_table.reshape(DEC_BATCH, n_pages).astype(jnp.int32)
    return {
        "x_prompt": nrm((BATCH, SEQ, D_MODEL), 1.0),
        "x_sample": nrm((DEC_BATCH, DEC_SEQ, D_MODEL), 1.0),
        "c_prompt": nrm((BATCH, D_MODEL), 1.0),
        "c_sample": nrm((DEC_BATCH, D_MODEL), 1.0),
        "cache_ckv": nrm((DEPTH, n_pool, PAGE_SIZE, KV_RANK), 1.0),
        "cache_kpe": nrm((DEPTH, n_pool, PAGE_SIZE, QK_ROPE), 1.0),
        "state_wkv": nrm((DEPTH, DEC_BATCH, RWKV_HEADS, RWKV_HEAD, RWKV_HEAD), 0.3),
        "state_shift": nrm((DEPTH, DEC_BATCH, RWKV_PROJ), 1.0),
        "page_table": page_table,
        "w_ada": nrm((DEPTH, D_MODEL, 6 * D_MODEL), 0.5 * D_MODEL ** -0.5),
        "b_ada": nrm((DEPTH, 6 * D_MODEL), 0.02),
        "g_pre_mix": gain((DEPTH, D_MODEL)),
        "g_post_mix": gain((DEPTH, D_MODEL)),
        "g_pre_ff": gain((DEPTH, D_MODEL)),
        "g_post_ff": gain((DEPTH, D_MODEL)),
        "w_in": nrm((DEPTH, D_MODEL, P_IN), D_MODEL ** -0.5),
        "mu_shift": unif((DEPTH, RWKV_PROJ), 0.0, 1.0),
        "g_q_norm": gain((DEPTH, Q_RANK)),
        "w_uq": nrm((DEPTH, Q_RANK, MLA_HEADS * (QK_NOPE + QK_ROPE)), Q_RANK ** -0.5),
        "g_kv_norm": gain((DEPTH, KV_RANK)),
        "w_ukv": nrm((DEPTH, KV_RANK, MLA_HEADS * (QK_NOPE + V_HEAD)), KV_RANK ** -0.5),
        "rw_w0": unif((DEPTH, RWKV_WIDTH), -3.0, 1.0),
        "rw_w2": nrm((DEPTH, DECAY_LORA, RWKV_WIDTH), 0.5 * DECAY_LORA ** -0.5),
        "rw_a0": nrm((DEPTH, RWKV_WIDTH), 0.5),
        "rw_a2": nrm((DEPTH, AAA_LORA, RWKV_WIDTH), AAA_LORA ** -0.5),
        "rw_g2": nrm((DEPTH, GATE_LORA, RWKV_WIDTH), GATE_LORA ** -0.5),
        "rw_k_k": 0.85 + nrm((DEPTH, RWKV_WIDTH), 0.05),
        "rw_k_a": gain((DEPTH, RWKV_WIDTH)),
        "rw_r_k": nrm((DEPTH, RWKV_HEADS, RWKV_HEAD), 0.1),
        "rw_lnx_w": gain((DEPTH, RWKV_WIDTH)),
        "rw_lnx_b": nrm((DEPTH, RWKV_WIDTH), 0.02),
        "w_branch_mla": nrm((DEPTH, MLA_WIDTH, D_MODEL), MLA_WIDTH ** -0.5),
        "w_branch_rwkv": nrm((DEPTH, RWKV_WIDTH, D_MODEL), RWKV_WIDTH ** -0.5),
        "w_out": nrm((DEPTH, D_MODEL, D_MODEL), D_MODEL ** -0.5),
        "w_ff1": nrm((DEPTH, D_MODEL, D_FF), D_MODEL ** -0.5),
        "w_ff2": nrm((DEPTH, D_FF, D_MODEL), D_FF ** -0.5),
    }


def reference(x_prompt, x_sample, c_prompt, c_sample, cache_ckv, cache_kpe, state_wkv, state_shift,
              page_table, w_ada, b_ada, g_pre_mix, g_post_mix, g_pre_ff, g_post_ff, w_in, mu_shift,
              g_q_norm, w_uq, g_kv_norm, w_ukv, rw_w0, rw_w2, rw_a0, rw_a2, rw_g2, rw_k_k, rw_k_a,
              rw_r_k, rw_lnx_w, rw_lnx_b, w_branch_mla, w_branch_rwkv, w_out, w_ff1, w_ff2):
    weights = (w_ada, b_ada, g_pre_mix, g_post_mix, g_pre_ff, g_post_ff, w_in, mu_shift,
               g_q_norm, w_uq, g_kv_norm, w_ukv, rw_w0, rw_w2, rw_a0, rw_a2, rw_g2, rw_k_k,
               rw_k_a, rw_r_k, rw_lnx_w, rw_lnx_b, w_branch_mla, w_branch_rwkv, w_out, w_ff1, w_ff2)
    b_p, s_p = x_prompt.shape[:2]
    b_s, s_s = x_sample.shape[:2]
    past_len = page_table.shape[1] * cache_ckv.shape[2]

    pos_p = jnp.arange(s_p)
    shift0_p = jnp.zeros((DEPTH, b_p, RWKV_PROJ), x_prompt.dtype)
    wkv0_p = jnp.zeros((DEPTH, b_p, RWKV_HEADS, RWKV_HEAD, RWKV_HEAD), x_prompt.dtype)

    def attend_prompt(l, q_nope, q_pe, c_kv, k_pe, w_ukv_l):
        return mla_prompt_attention(q_nope, q_pe, c_kv, k_pe, w_ukv_l)

    y_prompt, p_ckv, p_kpe, p_wkv, p_shift = run_trunk(
        x_prompt, c_prompt, pos_p, shift0_p, wkv0_p, attend_prompt, *weights)

    pos_s = past_len + jnp.arange(s_s)

    def attend_sample(l, q_nope, q_pe, c_kv, k_pe, w_ukv_l):
        return mla_sample_attention(q_nope, q_pe, c_kv, k_pe, cache_ckv, cache_kpe, l, page_table, w_ukv_l)

    y_sample, s_ckv, s_kpe, s_wkv, s_shift = run_trunk(
        x_sample, c_sample, pos_s, state_shift, state_wkv, attend_sample, *weights)

    return (y_prompt, y_sample, p_ckv, p_kpe, p_wkv, p_shift, s_ckv, s_kpe, s_wkv, s_shift)
```

```python
import functools

import jax
import jax.numpy as jnp
from jax import lax
from jax.experimental import pallas as pl
from jax.experimental.pallas import tpu as pltpu

F32 = jnp.float32
BF16 = jnp.bfloat16

D_MODEL = 1024
DEPTH = 4
MLA_HEADS = 8
QK_NOPE = 64
QK_ROPE = 32
V_HEAD = 64
Q_RANK = 384
KV_RANK = 256
MLA_WIDTH = MLA_HEADS * V_HEAD
MLA_PROJ = Q_RANK + KV_RANK + QK_ROPE
ATTN_SCALE = (QK_NOPE + QK_ROPE) ** -0.5
ROPE_BASE = 10000.0
RWKV_HEADS = 8
RWKV_HEAD = 64
RWKV_WIDTH = RWKV_HEADS * RWKV_HEAD
DECAY_LORA = 64
AAA_LORA = 64
GATE_LORA = 128
RWKV_PROJ = 3 * RWKV_WIDTH + DECAY_LORA + AAA_LORA + GATE_LORA
LNX_EPS = 64e-5
D_FF = 4 * D_MODEL
NORM_EPS = 1e-6

LANES = 128
HEAD_SLAB = LANES
ROPE_LANE0 = QK_NOPE
HALF_ROPE = QK_ROPE // 2
VMEM_LIMIT_BYTES = 56 * 1024 * 1024

ROW_TILE = 256
FFN_ROW_TILE = 512
FFN_CHUNK = 1024
ATTN_TILE = 512
SCAN_CHUNK = 64
DECODE_PAGES_PER_STEP = 8


def _params(*sem):
    return pltpu.CompilerParams(dimension_semantics=sem, vmem_limit_bytes=VMEM_LIMIT_BYTES)


def _bdot(a, b):
    return jnp.dot(a.astype(BF16), b.astype(BF16), preferred_element_type=F32)


def _bdot_nt(a, b):
    return lax.dot_general(a.astype(BF16), b.astype(BF16), (((1,), (1,)), ((), ())),
                           preferred_element_type=F32)


def _bdot_tn(a, b):
    return lax.dot_general(a.astype(BF16), b.astype(BF16), (((0,), (0,)), ((), ())),
                           preferred_element_type=F32)


def _rms(x, g):
    return x * lax.rsqrt(jnp.mean(x * x, axis=-1, keepdims=True) + NORM_EPS) * g


def _split_dot(x, ones_rhs, parts):
    acc = None
    rem = x
    for _ in range(parts):
        piece = rem.astype(BF16)
        rem = rem - piece.astype(F32)
        d = jnp.dot(piece, ones_rhs, preferred_element_type=F32)
        acc = d if acc is None else acc + d
    return acc


def _ones_split_dot(ones_lhs, x, parts):
    acc = None
    rem = x
    for _ in range(parts):
        piece = rem.astype(BF16)
        rem = rem - piece.astype(F32)
        d = jnp.dot(ones_lhs, piece, preferred_element_type=F32)
        acc = d if acc is None else acc + d
    return acc


def _const_spec(shape):
    nd = len(shape)
    return pl.BlockSpec(shape, lambda *_: (0,) * nd)


def _mod_spec(mod, tm, tiles_per_group):
    _, r, d = mod.shape
    if r == 1:
        return pl.BlockSpec((None, 1, d), lambda i: (i // tiles_per_group, 0, 0))
    return pl.BlockSpec((None, tm, d), lambda i: (i // tiles_per_group, i % tiles_per_group, 0))


def _ada_kernel(c_ref, w_ref, b_ref, o_ref):
    c = c_ref[...]
    o_ref[...] = _bdot(c * jax.nn.sigmoid(c), w_ref[...]) + b_ref[...]


def _adaln(c_all, w_ada, b_ada):
    depth, d, n = w_ada.shape
    rows = c_all.shape[0]
    tn = n // 4
    return pl.pallas_call(
        _ada_kernel,
        grid=(depth, n // tn),
        in_specs=[pl.BlockSpec((rows, d), lambda l, j: (0, 0)),
                  pl.BlockSpec((None, d, tn), lambda l, j: (l, 0, j)),
                  pl.BlockSpec((None, 1, tn), lambda l, j: (l, 0, j))],
        out_specs=pl.BlockSpec((None, rows, tn), lambda l, j: (l, 0, j)),
        out_shape=jax.ShapeDtypeStruct((depth, rows, n), F32),
        compiler_params=_params("arbitrary", "arbitrary"),
        name="adaln",
    )(c_all, w_ada, b_ada.reshape(depth, 1, n))


def _rope_slab(x, cos, sin_lo, sin_hi):
    return (x * cos + pltpu.roll(x, LANES - HALF_ROPE, 1) * sin_lo
            + pltpu.roll(x, HALF_ROPE, 1) * sin_hi)


def _inproj_kernel(x_ref, sh_ref, sc_ref, gpre_ref, cos_ref, slo_ref, shi_ref,
                   wq_ref, wckv_ref, wkpe_ref, wrw_ref, wgate_ref, gq_ref, wuq_ref,
                   gkv_ref, wukn_ref, wuv_ref,
                   q_ref, k_ref, v_ref, ckv_ref, kpe_ref, zrw_ref, gate_ref):
    x = x_ref[...]
    h = _rms(x, gpre_ref[...]) * (1.0 + sc_ref[...]) + sh_ref[...]
    hb = h.astype(BF16)
    zrw_ref[...] = jnp.dot(hb, wrw_ref[...], preferred_element_type=F32)
    gate_ref[...] = jax.nn.sigmoid(jnp.dot(hb, wgate_ref[...], preferred_element_type=F32))
    cos, slo, shi = cos_ref[...], slo_ref[...], shi_ref[...]

    q_lat = jnp.dot(hb, wq_ref[...], preferred_element_type=F32)
    q = _bdot(_rms(q_lat, gq_ref[...]), wuq_ref[...])
    for hd in range(MLA_HEADS):
        sl = slice(hd * HEAD_SLAB, (hd + 1) * HEAD_SLAB)
        q_ref[:, sl] = _rope_slab(q[:, sl], cos, slo, shi).astype(BF16)

    ckv = _rms(jnp.dot(hb, wckv_ref[...], preferred_element_type=F32), gkv_ref[...])
    ckv_ref[...] = ckv
    cb = ckv.astype(BF16)
    kpe = _rope_slab(jnp.dot(hb, wkpe_ref[...], preferred_element_type=F32), cos, slo, shi)
    kpe_ref[...] = kpe[:, ROPE_LANE0:ROPE_LANE0 + QK_ROPE]
    kn = jnp.dot(cb, wukn_ref[...], preferred_element_type=F32)
    for hd in range(MLA_HEADS):
        sl = slice(hd * HEAD_SLAB, (hd + 1) * HEAD_SLAB)
        k_ref[:, sl] = (kn[:, sl] + kpe).astype(BF16)
    v_ref[...] = jnp.dot(cb, wuv_ref[...], preferred_element_type=F32).astype(BF16)


def _inproj(x, sh, sc, g_pre, rope_tabs, lw, rows_per_group):
    t, d = x.shape
    tm = min(ROW_TILE, rows_per_group)
    tpg = rows_per_group // tm
    cos, slo, shi = rope_tabs
    tab_spec = pl.BlockSpec((tm, LANES), lambda i: (i % tpg, 0))
    row = lambda n: pl.BlockSpec((tm, n), lambda i: (i, 0))
    weights = [lw["w_q"], lw["w_ckv"], lw["w_kpe"], lw["w_rw"], lw["w_gate"], lw["g_q"],
               lw["w_uq"], lw["g_kv"], lw["w_ukn"], lw["w_uv"]]
    slab = MLA_HEADS * HEAD_SLAB
    return pl.pallas_call(
        _inproj_kernel,
        grid=(t // tm,),
        in_specs=[row(d), _mod_spec(sh, tm, tpg), _mod_spec(sc, tm, tpg), _const_spec(g_pre.shape),
                  tab_spec, tab_spec, tab_spec] + [_const_spec(w.shape) for w in weights],
        out_specs=[row(slab), row(slab), row(MLA_WIDTH), row(KV_RANK), row(QK_ROPE),
                   row(RWKV_PROJ), row(2 * d)],
        out_shape=[jax.ShapeDtypeStruct((t, slab), BF16), jax.ShapeDtypeStruct((t, slab), BF16),
                   jax.ShapeDtypeStruct((t, MLA_WIDTH), BF16), jax.ShapeDtypeStruct((t, KV_RANK), F32),
                   jax.ShapeDtypeStruct((t, QK_ROPE), F32), jax.ShapeDtypeStruct((t, RWKV_PROJ), F32),
                   jax.ShapeDtypeStruct((t, 2 * d), F32)],
        compiler_params=_params("arbitrary"),
        name="inproj",
    )(x, sh, sc, g_pre, cos, slo, shi, *weights)


def _attn_kernel(q_ref, k_ref, v_ref, o_ref, m_sc, l_sc, acc_sc, *, tile):
    i = pl.program_id(1)
    j = pl.program_id(2)

    @pl.when(j == 0)
    def _():
        m_sc[...] = jnp.full(m_sc.shape, -jnp.inf, F32)
        l_sc[...] = jnp.zeros(l_sc.shape, F32)
        acc_sc[...] = jnp.zeros(acc_sc.shape, F32)

    @pl.when(j <= i)
    def _():
        row = i * tile + lax.broadcasted_iota(jnp.int32, (tile, tile), 0)
        col = j * tile + lax.broadcasted_iota(jnp.int32, (tile, tile), 1)
        causal = col <= row
        low_half = lax.broadcasted_iota(jnp.int32, (tile, LANES), 1) < V_HEAD
        for pair in range(MLA_HEADS // 2):
            alphas, pvs = [], []
            vp = v_ref[:, pair * LANES:(pair + 1) * LANES]
            for hh in range(2):
                hd = 2 * pair + hh
                sl = slice(hd * HEAD_SLAB, (hd + 1) * HEAD_SLAB)
                s = _bdot_nt(q_ref[:, sl], k_ref[:, sl]) * ATTN_SCALE
                s = jnp.where(causal, s, -jnp.inf)
                m_prev = m_sc[hd]
                m_new = jnp.maximum(m_prev, jnp.max(s, axis=-1, keepdims=True))
                alpha = jnp.exp(m_prev - m_new)
                p = jnp.exp(s - m_new)
                l_sc[hd] = alpha * l_sc[hd] + jnp.sum(p, axis=-1, keepdims=True)
                m_sc[hd] = m_new
                alphas.append(alpha)
                pvs.append(jnp.dot(p.astype(BF16), vp, preferred_element_type=F32))
            sl = slice(pair * LANES, (pair + 1) * LANES)
            acc_sc[:, sl] = (jnp.where(low_half, alphas[0], alphas[1]) * acc_sc[:, sl]
                             + jnp.where(low_half, pvs[0], pvs[1]))

    @pl.when(j == i)
    def _():
        low_half = lax.broadcasted_iota(jnp.int32, (tile, LANES), 1) < V_HEAD
        for pair in range(MLA_HEADS // 2):
            sl = slice(pair * LANES, (pair + 1) * LANES)
            inv = jnp.where(low_half, 1.0 / l_sc[2 * pair], 1.0 / l_sc[2 * pair + 1])
            o_ref[:, sl] = (acc_sc[:, sl] * inv).astype(o_ref.dtype)


def _prompt_attention(q, k, v, batch, seq):
    tile = min(ATTN_TILE, seq)
    nt = seq // tile
    slab = MLA_HEADS * HEAD_SLAB
    q3, k3, v3 = (a.reshape(batch, seq, a.shape[-1]) for a in (q, k, v))
    out = pl.pallas_call(
        functools.partial(_attn_kernel, tile=tile),
        grid=(batch, nt, nt),
        in_specs=[pl.BlockSpec((None, tile, slab), lambda b, i, j: (b, i, 0)),
                  pl.BlockSpec((None, tile, slab), lambda b, i, j: (b, jnp.minimum(i, j), 0)),
                  pl.BlockSpec((None, tile, MLA_WIDTH), lambda b, i, j: (b, jnp.minimum(i, j), 0))],
        out_specs=pl.BlockSpec((None, tile, MLA_WIDTH), lambda b, i, j: (b, i, 0)),
        out_shape=jax.ShapeDtypeStruct((batch, seq, MLA_WIDTH), BF16),
        scratch_shapes=[pltpu.VMEM((MLA_HEADS, tile, 1), F32), pltpu.VMEM((MLA_HEADS, tile, 1), F32),
                        pltpu.VMEM((tile, MLA_WIDTH), F32)],
        compiler_params=_params("arbitrary", "arbitrary", "arbitrary"),
        name="prompt_attention",
    )(q3, k3, v3)
    return out.reshape(batch * seq, MLA_WIDTH)


def _qlat_kernel(q_ref, wuk_ref, o_ref):
    for hd in range(MLA_HEADS):
        o_ref[hd] = jnp.dot(q_ref[:, hd * HEAD_SLAB:(hd + 1) * HEAD_SLAB], wuk_ref[hd],
                            preferred_element_type=F32)


def _decode_kernel(pt_ref, qlat_ref, qpe_ref, cnew_ref, knew_ref, *rest, pages):
    ckv_refs, kpe_refs = rest[:pages], rest[pages:2 * pages]
    o_ref, m_sc, l_sc, acc_sc = rest[2 * pages:]
    j = pl.program_id(1)

    @pl.when(j == 0)
    def _():
        m_sc[...] = jnp.full(m_sc.shape, -jnp.inf, F32)
        l_sc[...] = jnp.zeros(l_sc.shape, F32)
        acc_sc[...] = jnp.zeros(acc_sc.shape, F32)

    ql = qlat_ref[...]
    qp = qpe_ref[...]
    qlb, qpb = ql.astype(BF16), qp.astype(BF16)
    cks, scores = [], []
    for pg in range(pages):
        ck = ckv_refs[pg][...].astype(BF16)
        kp = kpe_refs[pg][...].astype(BF16)
        cks.append(ck)
        scores.append((_bdot_nt(qlb, ck) + _bdot_nt(qpb, kp)) * ATTN_SCALE)
    m_prev = m_sc[...]
    m_new = m_prev
    for s in scores:
        m_new = jnp.maximum(m_new, jnp.max(s, axis=-1, keepdims=True))
    alpha = jnp.exp(m_prev - m_new)
    l_new = alpha * l_sc[...]
    acc = alpha * acc_sc[...]
    for s, ck in zip(scores, cks):
        p = jnp.exp(s - m_new)
        l_new = l_new + jnp.sum(p, axis=-1, keepdims=True)
        acc = acc + jnp.dot(p.astype(BF16), ck, preferred_element_type=F32)
    m_sc[...] = m_new
    l_sc[...] = l_new
    acc_sc[...] = acc

    @pl.when(j == pl.num_programs(1) - 1)
    def _():
        cn, kn = cnew_ref[...], knew_ref[...]
        s_cur = (jnp.sum(ql * cn, axis=-1, keepdims=True)
                 + jnp.sum(qp * kn, axis=-1, keepdims=True)) * ATTN_SCALE
        m_fin = jnp.maximum(m_new, s_cur)
        a = jnp.exp(m_new - m_fin)
        pc = jnp.exp(s_cur - m_fin)
        o_ref[...] = (a * acc + pc * cn) / (a * l_new + pc)


def _olat_kernel(o_ref, wuv_ref, y_ref):
    acc = None
    for hd in range(MLA_HEADS):
        d = _bdot(o_ref[hd], wuv_ref[hd])
        acc = d if acc is None else acc + d
    y_ref[...] = acc.astype(y_ref.dtype)


def _sample_attention(layer, q_slab, ckv_new, kpe_new, cache_ckv, cache_kpe, page_table, lw):
    b = q_slab.shape[0]
    n_pages = page_table.shape[1]
    page = cache_ckv.shape[2]
    pages = DECODE_PAGES_PER_STEP
    qlat = pl.pallas_call(
        _qlat_kernel,
        out_shape=jax.ShapeDtypeStruct((MLA_HEADS, b, KV_RANK), F32),
        compiler_params=pltpu.CompilerParams(vmem_limit_bytes=VMEM_LIMIT_BYTES),
        name="decode_qlat",
    )(q_slab, lw["w_uk_t"])
    qlat = qlat.transpose(1, 0, 2)
    qpe = q_slab.reshape(b, MLA_HEADS, HEAD_SLAB)[:, :, ROPE_LANE0:ROPE_LANE0 + QK_ROPE].astype(F32)

    def page_spec(width, pg):
        return pl.BlockSpec((None, None, page, width),
                            lambda bi, j, pt: (layer, pt[bi * n_pages + j * pages + pg], 0, 0))

    grid_spec = pltpu.PrefetchScalarGridSpec(
        num_scalar_prefetch=1,
        grid=(b, n_pages // pages),
        in_specs=[pl.BlockSpec((None, MLA_HEADS, KV_RANK), lambda bi, j, pt: (bi, 0, 0)),
                  pl.BlockSpec((None, MLA_HEADS, QK_ROPE), lambda bi, j, pt: (bi, 0, 0)),
                  pl.BlockSpec((None, 1, KV_RANK), lambda bi, j, pt: (bi, 0, 0)),
                  pl.BlockSpec((None, 1, QK_ROPE), lambda bi, j, pt: (bi, 0, 0))]
                 + [page_spec(KV_RANK, pg) for pg in range(pages)]
                 + [page_spec(QK_ROPE, pg) for pg in range(pages)],
        out_specs=pl.BlockSpec((None, MLA_HEADS, KV_RANK), lambda bi, j, pt: (bi, 0, 0)),
        scratch_shapes=[pltpu.VMEM((MLA_HEADS, 1), F32), pltpu.VMEM((MLA_HEADS, 1), F32),
                        pltpu.VMEM((MLA_HEADS, KV_RANK), F32)],
    )
    olat = pl.pallas_call(
        functools.partial(_decode_kernel, pages=pages),
        grid_spec=grid_spec,
        out_shape=jax.ShapeDtypeStruct((b, MLA_HEADS, KV_RANK), F32),
        compiler_params=_params("arbitrary", "arbitrary"),
        name="decode_attention",
    )(page_table.reshape(-1), qlat, qpe, ckv_new.reshape(b, 1, KV_RANK), kpe_new.reshape(b, 1, QK_ROPE),
      *([cache_ckv] * pages), *([cache_kpe] * pages))
    return pl.pallas_call(
        _olat_kernel,
        out_shape=jax.ShapeDtypeStruct((b, MLA_WIDTH), BF16),
        compiler_params=pltpu.CompilerParams(vmem_limit_bytes=VMEM_LIMIT_BYTES),
        name="decode_out",
    )(olat.transpose(1, 0, 2), lw["w_uv_wide"])


def _softplus(x):
    return jnp.maximum(x, 0.0) + jnp.log(1.0 + jnp.exp(-jnp.abs(x)))


def _rwkv_prep_kernel(z_ref, s0_ref, mu_ref, w0_ref, w2_ref, a0_ref, a2_ref, g2_ref, kk_ref, ka_ref,
                      rk_ref, seg_ref,
                      r_ref, lw_ref, k_ref, v_ref, an_ref, bn_ref, g_ref, bonus_ref, last_ref,
                      carry_sc, *, tiles_per_seq, single_token):
    z = z_ref[...]
    tm = z.shape[0]
    if single_token:
        z_prev = s0_ref[...]
    else:
        i = pl.program_id(0)
        first = jnp.where(i % tiles_per_seq == 0, s0_ref[...], carry_sc[...])
        is_row0 = lax.broadcasted_iota(jnp.int32, z.shape, 0) == 0
        z_prev = jnp.where(is_row0, first, pltpu.roll(z, 1, 0))
        carry_sc[...] = z[tm - 1:tm, :]
    last_ref[...] = z[tm - 1:tm, :]
    zm = z + mu_ref[...] * (z_prev - z)
    w = RWKV_WIDTH
    r, k, v = zm[:, :w], zm[:, w:2 * w], zm[:, 2 * w:3 * w]
    xwa = zm[:, 3 * w:3 * w + DECAY_LORA + AAA_LORA]
    xg = zm[:, 3 * w + DECAY_LORA + AAA_LORA:]
    w_log = -_softplus(-(w0_ref[...] + _bdot(jnp.tanh(xwa), w2_ref[...]))) - 0.5
    a = jax.nn.sigmoid(a0_ref[...] + _bdot(xwa, a2_ref[...]))
    g_ref[...] = _bdot(jax.nn.sigmoid(xg), g2_ref[...])
    seg = seg_ref[...]
    kk = k * kk_ref[...]
    kk = kk / jnp.maximum(jnp.sqrt(_split_dot(kk * kk, seg, 2)), 1e-12)
    k2 = k * (1.0 + (a - 1.0) * ka_ref[...])
    r_ref[...] = r
    lw_ref[...] = -jnp.exp(w_log)
    k_ref[...] = k2
    v_ref[...] = v
    an_ref[...] = -kk
    bn_ref[...] = kk * a
    bonus_ref[...] = _split_dot(r * k2 * rk_ref[...], seg, 2) * v


def _rwkv_prep(z_rw, shift0, lw, rows_per_group, single_token):
    t = z_rw.shape[0]
    tm = min(ROW_TILE, t) if single_token else min(ROW_TILE, rows_per_group)
    tps = 1 if single_token else rows_per_group // tm
    n_seq = t // rows_per_group
    row = lambda n: pl.BlockSpec((tm, n), lambda i: (i, 0))
    if single_token:
        s0, s0_spec = shift0, row(RWKV_PROJ)
    else:
        s0 = shift0.reshape(n_seq, 1, RWKV_PROJ)
        s0_spec = pl.BlockSpec((None, 1, RWKV_PROJ), lambda i: (i // tps, 0, 0))
    consts = [lw["mu"], lw["rw_w0"], lw["rw_w2"], lw["rw_a0"], lw["rw_a2"], lw["rw_g2"], lw["rw_k_k"],
              lw["rw_k_a"], lw["rw_r_k"], lw["seg_ones"]]
    wide = jax.ShapeDtypeStruct((t, RWKV_WIDTH), F32)
    outs = pl.pallas_call(
        functools.partial(_rwkv_prep_kernel, tiles_per_seq=tps, single_token=single_token),
        grid=(t // tm,),
        in_specs=[row(RWKV_PROJ), s0_spec] + [_const_spec(c.shape) for c in consts],
        out_specs=[row(RWKV_WIDTH)] * 8 + [pl.BlockSpec((None, 1, RWKV_PROJ), lambda i: (i // tps, 0, 0))],
        out_shape=[wide] * 8 + [jax.ShapeDtypeStruct((max(n_seq, 1), 1, RWKV_PROJ), F32)],
        scratch_shapes=[pltpu.VMEM((1, RWKV_PROJ), F32)],
        compiler_params=_params("arbitrary"),
        name="rwkv_prep",
    )(z_rw, s0, *consts)
    return outs[:8], outs[8]


def _head_norm_gate(y, hd, lnw_ref, lnb_ref, bonus, g):
    sl = slice(hd * RWKV_HEAD, (hd + 1) * RWKV_HEAD)
    mean = jnp.mean(y, axis=-1, keepdims=True)
    cen = y - mean
    var = jnp.mean(cen * cen, axis=-1, keepdims=True)
    yn = cen * lax.rsqrt(var + LNX_EPS) * lnw_ref[:, sl] + lnb_ref[:, sl]
    return (yn + bonus[:, sl]) * g[:, sl]


def _scan_kernel(r_ref, lw_ref, k_ref, v_ref, an_ref, bn_ref, g_ref, bonus_ref, s0_ref, lnw_ref, lnb_ref,
                 y_ref, sout_ref, state_sc, *, chunk):
    c = pl.program_id(1)

    @pl.when(c == 0)
    def _():
        state_sc[...] = s0_ref[...]

    ri = lax.broadcasted_iota(jnp.int32, (chunk, chunk), 0)
    ci = lax.broadcasted_iota(jnp.int32, (chunk, chunk), 1)
    strict, incl = ci < ri, ci <= ri
    lw = lw_ref[...]
    cum = _ones_split_dot(jnp.where(incl, 1.0, 0.0).astype(BF16), lw, 3)
    cum_last = cum[chunk - 1:chunk, :]
    e_cum, e_neg, e_tail = jnp.exp(cum), jnp.exp(-cum), jnp.exp(cum_last - cum)
    an, bn, k, v = an_ref[...], bn_ref[...], k_ref[...], v_ref[...]
    a_t = an * jnp.exp(cum - lw)
    r_t = r_ref[...] * e_cum
    b_t, k_t = bn * e_neg, k * e_neg
    b_h, k_h = bn * e_tail, k * e_tail
    e_last = jnp.exp(cum_last)
    bonus, g = bonus_ref[...], g_ref[...]
    for hd in range(RWKV_HEADS):
        sl = slice(hd * RWKV_HEAD, (hd + 1) * RWKV_HEAD)
        state = state_sc[hd]
        ar = jnp.concatenate([a_t[:, sl], r_t[:, sl]], axis=0)
        ar_b = _bdot_nt(ar, b_t[:, sl])
        ar_k = _bdot_nt(ar, k_t[:, sl])
        ar_s = _bdot_nt(ar, state)
        vh = v[:, sl]
        l_ab = jnp.where(strict, ar_b[:chunk], 0.0)
        l_ak = jnp.where(strict, ar_k[:chunk], 0.0)
        m_rb = jnp.where(incl, ar_b[chunk:], 0.0)
        m_rk = jnp.where(incl, ar_k[chunk:], 0.0)
        u = ar_s[:chunk] + _bdot(l_ak, vh)
        lp = l_ab
        span = 1
        while span < chunk:
            u = u + _bdot(lp, u)
            span *= 2
            if span < chunk:
                lp = _bdot(lp, lp)
        y = ar_s[chunk:] + _bdot(m_rb, u) + _bdot(m_rk, vh)
        uv = jnp.concatenate([u, vh], axis=0)
        bk = jnp.concatenate([b_h[:, sl], k_h[:, sl]], axis=0)
        state_sc[hd] = state * e_last[:, sl] + _bdot_tn(uv, bk)
        y_ref[:, sl] = _head_norm_gate(y, hd, lnw_ref, lnb_ref, bonus, g).astype(y_ref.dtype)

    @pl.when(c == pl.num_programs(1) - 1)
    def _():
        sout_ref[...] = state_sc[...]


def _rwkv_scan(prep, state0, lw, batch, seq):
    chunk = min(SCAN_CHUNK, seq)
    nc = seq // chunk
    tok = pl.BlockSpec((chunk, RWKV_WIDTH), lambda b, c: (b * nc + c, 0))
    st = pl.BlockSpec((None, RWKV_HEADS, RWKV_HEAD, RWKV_HEAD), lambda b, c: (b, 0, 0, 0))
    return pl.pallas_call(
        functools.partial(_scan_kernel, chunk=chunk),
        grid=(batch, nc),
        in_specs=[tok] * 8 + [st, _const_spec(lw["lnx_w"].shape), _const_spec(lw["lnx_b"].shape)],
        out_specs=[tok, st],
        out_shape=[jax.ShapeDtypeStruct((batch * seq, RWKV_WIDTH), BF16),
                   jax.ShapeDtypeStruct(state0.shape, F32)],
        scratch_shapes=[pltpu.VMEM((RWKV_HEADS, RWKV_HEAD, RWKV_HEAD), F32)],
        compiler_params=_params("arbitrary", "arbitrary"),
        name="rwkv_scan",
    )(*prep, state0, lw["lnx_w"], lw["lnx_b"])


def _wkv_step_kernel(r_ref, lw_ref, k_ref, v_ref, an_ref, bn_ref, g_ref, bonus_ref, s_ref, lnw_ref, lnb_ref,
                     y_ref, sout_ref):
    n = RWKV_HEAD
    eye = (lax.broadcasted_iota(jnp.int32, (n, n), 0) == lax.broadcasted_iota(jnp.int32, (n, n), 1))
    bonus, g = bonus_ref[...], g_ref[...]
    for hd in range(RWKV_HEADS):
        sl = slice(hd * n, (hd + 1) * n)
        state = s_ref[hd]
        sa = jnp.sum(state * an_ref[:, sl], axis=-1, keepdims=True)
        v_col = jnp.sum(jnp.where(eye, v_ref[:, sl], 0.0), axis=-1, keepdims=True)
        new = state * jnp.exp(lw_ref[:, sl]) + sa * bn_ref[:, sl] + v_col * k_ref[:, sl]
        sout_ref[hd] = new
        y_col = jnp.sum(new * r_ref[:, sl], axis=-1, keepdims=True)
        y_row = jnp.sum(jnp.where(eye, y_col, 0.0), axis=0, keepdims=True)
        y_ref[:, sl] = _head_norm_gate(y_row, hd, lnw_ref, lnb_ref, bonus, g).astype(y_ref.dtype)


def _wkv_step(prep, state, lw):
    b = state.shape[0]
    tok = pl.BlockSpec((None, 1, RWKV_WIDTH), lambda i: (i, 0, 0))
    st = pl.BlockSpec((None, RWKV_HEADS, RWKV_HEAD, RWKV_HEAD), lambda i: (i, 0, 0, 0))
    y, s_new = pl.pallas_call(
        _wkv_step_kernel,
        grid=(b,),
        in_specs=[tok] * 8 + [st, _const_spec(lw["lnx_w"].shape), _const_spec(lw["lnx_b"].shape)],
        out_specs=[tok, st],
        out_shape=[jax.ShapeDtypeStruct((b, 1, RWKV_WIDTH), F32), jax.ShapeDtypeStruct(state.shape, F32)],
        compiler_params=_params("arbitrary"),
        name="wkv_step",
    )(*[p.reshape(b, 1, RWKV_WIDTH) for p in prep], state, lw["lnx_w"], lw["lnx_b"])
    return y.reshape(b, RWKV_WIDTH), s_new


def _merge_kernel(x_ref, ym_ref, yr_ref, gate_ref, gt_ref, gpost_ref, wbm_ref, wbr_ref, wout_ref, o_ref):
    d = x_ref.shape[-1]
    merged = (gate_ref[:, :d] * _bdot(ym_ref[...], wbm_ref[...])
              + gate_ref[:, d:] * _bdot(yr_ref[...], wbr_ref[...]))
    out = _bdot(merged, wout_ref[...])
    o_ref[...] = x_ref[...] + gt_ref[...] * _rms(out, gpost_ref[...])


def _merge(x, y_mla, y_rw, gate, gt, g_post, lw, rows_per_group):
    t, d = x.shape
    tm = min(ROW_TILE, rows_per_group)
    tpg = rows_per_group // tm
    row = lambda n: pl.BlockSpec((tm, n), lambda i: (i, 0))
    weights = [lw["w_branch_mla"], lw["w_branch_rwkv"], lw["w_out"]]
    return pl.pallas_call(
        _merge_kernel,
        grid=(t // tm,),
        in_specs=[row(d), row(MLA_WIDTH), row(RWKV_WIDTH), row(2 * d), _mod_spec(gt, tm, tpg),
                  _const_spec(g_post.shape)] + [_const_spec(w.shape) for w in weights],
        out_specs=row(d),
        out_shape=jax.ShapeDtypeStruct((t, d), F32),
        compiler_params=_params("arbitrary"),
        name="merge",
    )(x, y_mla, y_rw, gate, gt, g_post, *weights)


def _ffn_kernel(x_ref, sh_ref, sc_ref, gt_ref, gpre_ref, gpost_ref, w1_ref, w2_ref, o_ref):
    x = x_ref[...]
    hb = (_rms(x, gpre_ref[...]) * (1.0 + sc_ref[...]) + sh_ref[...]).astype(BF16)
    acc = jnp.zeros(x.shape, F32)
    for c0 in range(0, w1_ref.shape[1], FFN_CHUNK):
        a = jnp.maximum(jnp.dot(hb, w1_ref[:, c0:c0 + FFN_CHUNK], preferred_element_type=F32), 0.0)
        acc = acc + _bdot(a * a, w2_ref[c0:c0 + FFN_CHUNK, :])
    o_ref[...] = x + gt_ref[...] * _rms(acc, gpost_ref[...])


def _ffn(x, sh, sc, gt, g_pre, g_post, lw, rows_per_group):
    t, d = x.shape
    tm = min(FFN_ROW_TILE, rows_per_group)
    tpg = rows_per_group // tm
    row = pl.BlockSpec((tm, d), lambda i: (i, 0))
    return pl.pallas_call(
        _ffn_kernel,
        grid=(t // tm,),
        in_specs=[row, _mod_spec(sh, tm, tpg), _mod_spec(sc, tm, tpg), _mod_spec(gt, tm, tpg),
                  _const_spec(g_pre.shape), _const_spec(g_post.shape),
                  _const_spec(lw["w_ff1"].shape), _const_spec(lw["w_ff2"].shape)],
        out_specs=row,
        out_shape=jax.ShapeDtypeStruct((t, d), F32),
        compiler_params=_params("arbitrary"),
        name="ffn",
    )(x, sh, sc, gt, g_pre, g_post, lw["w_ff1"], lw["w_ff2"])


def _rope_tables(pos):
    inv = ROPE_BASE ** (-jnp.arange(HALF_ROPE, dtype=F32) / HALF_ROPE)
    ang = pos.astype(F32)[:, None] * inv[None, :]
    cos, sin = jnp.cos(ang), jnp.sin(ang)
    n = pos.shape[0]
    ones = jnp.ones((n, ROPE_LANE0), F32)
    zeros = jnp.zeros((n, ROPE_LANE0), F32)
    pad = jnp.zeros((n, LANES - ROPE_LANE0 - QK_ROPE), F32)
    z16 = jnp.zeros((n, HALF_ROPE), F32)
    cos_t = jnp.concatenate([ones, cos, cos, pad], axis=1)
    sin_lo = jnp.concatenate([zeros, -sin, z16, pad], axis=1)
    sin_hi = jnp.concatenate([zeros, z16, sin, pad], axis=1)
    return cos_t, sin_lo, sin_hi


def _layer_weights(l, p):
    d = D_MODEL
    w_in = p["w_in"][l]
    w_q = w_in[:, :Q_RANK]
    w_ckv = w_in[:, Q_RANK:Q_RANK + KV_RANK]
    w_kpe = jnp.pad(w_in[:, Q_RANK + KV_RANK:MLA_PROJ], ((0, 0), (ROPE_LANE0, LANES - ROPE_LANE0 - QK_ROPE)))
    w_rw = w_in[:, MLA_PROJ:MLA_PROJ + RWKV_PROJ]
    w_gate = w_in[:, MLA_PROJ + RWKV_PROJ:]
    qk = QK_NOPE + QK_ROPE
    w_uq = jnp.pad(p["w_uq"][l].reshape(Q_RANK, MLA_HEADS, qk), ((0, 0), (0, 0), (0, HEAD_SLAB - qk)))
    w_ukv = p["w_ukv"][l].reshape(KV_RANK, MLA_HEADS, QK_NOPE + V_HEAD)
    w_uk, w_uv = w_ukv[..., :QK_NOPE], w_ukv[..., QK_NOPE:]
    w_ukn = jnp.pad(w_uk, ((0, 0), (0, 0), (0, HEAD_SLAB - QK_NOPE)))
    w_uk_t = jnp.pad(w_uk.transpose(1, 2, 0), ((0, 0), (0, HEAD_SLAB - QK_NOPE), (0, 0)))
    eye = jnp.eye(MLA_HEADS, dtype=F32)
    w_uv_wide = (w_uv.transpose(1, 0, 2)[:, :, None, :] * eye[:, None, :, None]).reshape(
        MLA_HEADS, KV_RANK, MLA_WIDTH)
    lora_pad_lo = ((0, AAA_LORA), (0, 0))
    lora_pad_hi = ((DECAY_LORA, 0), (0, 0))
    heads = jnp.arange(RWKV_WIDTH) // RWKV_HEAD
    row2 = lambda a: a.reshape(1, -1)
    return {
        "w_q": w_q.astype(BF16), "w_ckv": w_ckv.astype(BF16), "w_kpe": w_kpe.astype(BF16),
        "w_rw": w_rw.astype(BF16), "w_gate": w_gate.astype(BF16),
        "g_q": row2(p["g_q_norm"][l]), "g_kv": row2(p["g_kv_norm"][l]),
        "w_uq": w_uq.reshape(Q_RANK, MLA_HEADS * HEAD_SLAB).astype(BF16),
        "w_ukn": w_ukn.reshape(KV_RANK, MLA_HEADS * HEAD_SLAB).astype(BF16),
        "w_uv": w_uv.reshape(KV_RANK, MLA_WIDTH).astype(BF16),
        "w_uk_t": w_uk_t.astype(BF16), "w_uv_wide": w_uv_wide.astype(BF16),
        "mu": row2(p["mu_shift"][l]),
        "rw_w0": row2(p["rw_w0"][l]), "rw_a0": row2(p["rw_a0"][l]),
        "rw_w2": jnp.pad(p["rw_w2"][l], lora_pad_lo).astype(BF16),
        "rw_a2": jnp.pad(p["rw_a2"][l], lora_pad_hi).astype(BF16),
        "rw_g2": p["rw_g2"][l].astype(BF16),
        "rw_k_k": row2(p["rw_k_k"][l]), "rw_k_a": row2(p["rw_k_a"][l]), "rw_r_k": row2(p["rw_r_k"][l]),
        "seg_ones": (heads[:, None] == heads[None, :]).astype(BF16),
        "lnx_w": row2(p["rw_lnx_w"][l]), "lnx_b": row2(p["rw_lnx_b"][l]),
        "w_branch_mla": p["w_branch_mla"][l].astype(BF16), "w_branch_rwkv": p["w_branch_rwkv"][l].astype(BF16),
        "w_out": p["w_out"][l].astype(BF16),
        "w_ff1": p["w_ff1"][l].astype(BF16), "w_ff2": p["w_ff2"][l].astype(BF16),
        "g_pre_mix": row2(p["g_pre_mix"][l]), "g_post_mix": row2(p["g_post_mix"][l]),
        "g_pre_ff": row2(p["g_pre_ff"][l]), "g_post_ff": row2(p["g_post_ff"][l]),
    }


def _trunk(x, mods, rope_tabs, rows_per_group, layer_weights, mixers):
    ckv_rows, kpe_rows, wkv_states, shift_rows = [], [], [], []
    for l in range(DEPTH):
        lw = layer_weights[l]
        sh1, sc1, gt1, sh2, sc2, gt2 = mods[l]
        q, k, v, ckv, kpe, z_rw, gate = _inproj(x, sh1, sc1, lw["g_pre_mix"], rope_tabs, lw, rows_per_group)
        y_mla, y_rw, wkv_new, shift_new = mixers(l, lw, q, k, v, ckv, kpe, z_rw)
        x = _merge(x, y_mla, y_rw, gate, gt1, lw["g_post_mix"], lw, rows_per_group)
        x = _ffn(x, sh2, sc2, gt2, lw["g_pre_ff"], lw["g_post_ff"], lw, rows_per_group)
        ckv_rows.append(ckv)
        kpe_rows.append(kpe)
        wkv_states.append(wkv_new)
        shift_rows.append(shift_new)
    return x, jnp.stack(ckv_rows), jnp.stack(kpe_rows), jnp.stack(wkv_states), jnp.stack(shift_rows)


def kernel(x_prompt, x_sample, c_prompt, c_sample, cache_ckv, cache_kpe, state_wkv, state_shift, page_table, w_ada, b_ada, g_pre_mix, g_post_mix, g_pre_ff, g_post_ff, w_in, mu_shift, g_q_norm, w_uq, g_kv_norm, w_ukv, rw_w0, rw_w2, rw_a0, rw_a2, rw_g2, rw_k_k, rw_k_a, rw_r_k, rw_lnx_w, rw_lnx_b, w_branch_mla, w_branch_rwkv, w_out, w_ff1, w_ff2):
    params = dict(g_pre_mix=g_pre_mix, g_post_mix=g_post_mix, g_pre_ff=g_pre_ff, g_post_ff=g_post_ff,
                  w_in=w_in, mu_shift=mu_shift, g_q_norm=g_q_norm, w_uq=w_uq, g_kv_norm=g_kv_norm,
                  w_ukv=w_ukv, rw_w0=rw_w0, rw_w2=rw_w2, rw_a0=rw_a0, rw_a2=rw_a2, rw_g2=rw_g2,
                  rw_k_k=rw_k_k, rw_k_a=rw_k_a, rw_r_k=rw_r_k, rw_lnx_w=rw_lnx_w, rw_lnx_b=rw_lnx_b,
                  w_branch_mla=w_branch_mla, w_branch_rwkv=w_branch_rwkv, w_out=w_out,
                  w_ff1=w_ff1, w_ff2=w_ff2)
    b_p, s_p, d = x_prompt.shape
    b_s, s_s, _ = x_sample.shape
    assert s_s == 1, "the sample group advances one token per sequence"
    past_len = page_table.shape[1] * cache_ckv.shape[2]
    layer_weights = [_layer_weights(l, params) for l in range(DEPTH)]

    mod = _adaln(jnp.concatenate([c_prompt, c_sample], axis=0), w_ada, b_ada)
    mod = mod.reshape(DEPTH, b_p + b_s, 6, d)
    mods_p = [[mod[l, :b_p, j].reshape(b_p, 1, d) for j in range(6)] for l in range(DEPTH)]
    mods_s = [[mod[l, b_p:, j].reshape(1, b_s, d) for j in range(6)] for l in range(DEPTH)]

    shift0_p = jnp.zeros((b_p, RWKV_PROJ), F32)
    wkv0_p = jnp.zeros((b_p, RWKV_HEADS, RWKV_HEAD, RWKV_HEAD), F32)

    def prompt_mixers(l, lw, q, k, v, ckv, kpe, z_rw):
        y_mla = _prompt_attention(q, k, v, b_p, s_p)
        prep, last = _rwkv_prep(z_rw, shift0_p, lw, s_p, single_token=False)
        y_rw, wkv_new = _rwkv_scan(prep, wkv0_p, lw, b_p, s_p)
        return y_mla, y_rw, wkv_new, last.reshape(b_p, RWKV_PROJ)

    y_p, p_ckv, p_kpe, p_wkv, p_shift = _trunk(
        x_prompt.reshape(b_p * s_p, d), mods_p, _rope_tables(jnp.arange(s_p)), s_p, layer_weights,
        prompt_mixers)

    def sample_mixers(l, lw, q, k, v, ckv, kpe, z_rw):
        y_mla = _sample_attention(l, q, ckv, kpe, cache_ckv, cache_kpe, page_table, lw)
        prep, _ = _rwkv_prep(z_rw, state_shift[l], lw, b_s, single_token=True)
        y_rw, wkv_new = _wkv_step(prep, state_wkv[l], lw)
        return y_mla, y_rw, wkv_new, z_rw

    y_s, s_ckv, s_kpe, s_wkv, s_shift = _trunk(
        x_sample.reshape(b_s, d), mods_s, _rope_tables(jnp.full((b_s,), past_len)), b_s, layer_weights,
        sample_mixers)

    return (y_p.reshape(b_p, s_p, d), y_s.reshape(b_s, 1, d),
            p_ckv.reshape(DEPTH, b_p, s_p, KV_RANK), p_kpe.reshape(DEPTH, b_p, s_p, QK_ROPE), p_wkv, p_shift,
            s_ckv.reshape(DEPTH, b_s, 1, KV_RANK), s_kpe.reshape(DEPTH, b_s, 1, QK_ROPE), s_wkv, s_shift)
```

```python
import functools

import jax
import jax.numpy as jnp
from jax import lax
from jax.experimental import pallas as pl
from jax.experimental.pallas import tpu as pltpu

F32 = jnp.float32
BF16 = jnp.bfloat16

D_MODEL = 1024
DEPTH = 4
MLA_HEADS = 8
QK_NOPE = 64
QK_ROPE = 32
V_HEAD = 64
Q_RANK = 384
KV_RANK = 256
MLA_WIDTH = MLA_HEADS * V_HEAD
MLA_PROJ = Q_RANK + KV_RANK + QK_ROPE
ATTN_SCALE = (QK_NOPE + QK_ROPE) ** -0.5
ROPE_BASE = 10000.0
RWKV_HEADS = 8
RWKV_HEAD = 64
RWKV_WIDTH = RWKV_HEADS * RWKV_HEAD
RWKV_PAIRS = RWKV_HEADS // 2
DECAY_LORA = 64
AAA_LORA = 64
GATE_LORA = 128
RWKV_PROJ = 3 * RWKV_WIDTH + DECAY_LORA + AAA_LORA + GATE_LORA
LNX_EPS = 64e-5
D_FF = 4 * D_MODEL
NORM_EPS = 1e-6

LANES = 128
SUBLANES = 8
HEAD_SLAB = LANES
ROPE_LANE0 = QK_NOPE
HALF_ROPE = QK_ROPE // 2
VMEM_LIMIT_BYTES = 56 * 1024 * 1024

ROW_TILE = 256
FFN_ROW_TILE = 512
FFN_CHUNK = 1024
ATTN_TILE = 512
SCAN_CHUNK = 64
DECODE_SLOTS = 2
WKV_STEP_SEQS = 8


def _params(*sem):
    return pltpu.CompilerParams(dimension_semantics=sem, vmem_limit_bytes=VMEM_LIMIT_BYTES)


def _bdot(a, b):
    return jnp.dot(a.astype(BF16), b.astype(BF16), preferred_element_type=F32)


def _bdot_nt(a, b):
    return lax.dot_general(a.astype(BF16), b.astype(BF16), (((1,), (1,)), ((), ())),
                           preferred_element_type=F32)


def _bdot_tn(a, b):
    return lax.dot_general(a.astype(BF16), b.astype(BF16), (((0,), (0,)), ((), ())),
                           preferred_element_type=F32)


def _rms(x, g):
    return x * lax.rsqrt(jnp.mean(x * x, axis=-1, keepdims=True) + NORM_EPS) * g


def _split_dot(x, ones_rhs, parts):
    acc = None
    rem = x
    for _ in range(parts):
        piece = rem.astype(BF16)
        rem = rem - piece.astype(F32)
        d = jnp.dot(piece, ones_rhs, preferred_element_type=F32)
        acc = d if acc is None else acc + d
    return acc


def _ones_split_dot(ones_lhs, x, parts):
    acc = None
    rem = x
    for _ in range(parts):
        piece = rem.astype(BF16)
        rem = rem - piece.astype(F32)
        d = jnp.dot(ones_lhs, piece, preferred_element_type=F32)
        acc = d if acc is None else acc + d
    return acc


def _const_spec(shape):
    nd = len(shape)
    return pl.BlockSpec(shape, lambda *_: (0,) * nd)


def _mod_spec(mod, tm, tiles_per_group):
    _, r, d = mod.shape
    if r == 1:
        return pl.BlockSpec((None, 1, d), lambda i: (i // tiles_per_group, 0, 0))
    return pl.BlockSpec((None, tm, d), lambda i: (i // tiles_per_group, i % tiles_per_group, 0))


def _ada_kernel(c_ref, w_ref, b_ref, o_ref):
    c = c_ref[...]
    o_ref[...] = _bdot(c * jax.nn.sigmoid(c), w_ref[...]) + b_ref[...]


def _adaln(c_all, w_ada, b_ada):
    depth, d, n = w_ada.shape
    rows = c_all.shape[0]
    tn = n // 4
    return pl.pallas_call(
        _ada_kernel,
        grid=(depth, n // tn),
        in_specs=[pl.BlockSpec((rows, d), lambda l, j: (0, 0)),
                  pl.BlockSpec((None, d, tn), lambda l, j: (l, 0, j)),
                  pl.BlockSpec((None, 1, tn), lambda l, j: (l, 0, j))],
        out_specs=pl.BlockSpec((None, rows, tn), lambda l, j: (l, 0, j)),
        out_shape=jax.ShapeDtypeStruct((depth, rows, n), F32),
        compiler_params=_params("arbitrary", "arbitrary"),
        name="adaln",
    )(c_all, w_ada, b_ada.reshape(depth, 1, n))


def _rope_slab(x, cos, sin_lo, sin_hi):
    return (x * cos + pltpu.roll(x, LANES - HALF_ROPE, 1) * sin_lo
            + pltpu.roll(x, HALF_ROPE, 1) * sin_hi)


def _inproj_kernel(x_ref, sh_ref, sc_ref, gpre_ref, cos_ref, slo_ref, shi_ref,
                   wq_ref, wckv_ref, wkpe_ref, wrw_ref, wgate_ref, gq_ref, wuq_ref,
                   gkv_ref, wukn_ref, wuvt_ref,
                   q_ref, k_ref, vt_ref, ckv_ref, kpe_ref, zrw_ref, gate_ref):
    x = x_ref[...]
    h = _rms(x, gpre_ref[...]) * (1.0 + sc_ref[...]) + sh_ref[...]
    hb = h.astype(BF16)
    zrw_ref[...] = jnp.dot(hb, wrw_ref[...], preferred_element_type=F32)
    gate_ref[...] = jax.nn.sigmoid(jnp.dot(hb, wgate_ref[...], preferred_element_type=F32))
    cos, slo, shi = cos_ref[...], slo_ref[...], shi_ref[...]

    q_lat = jnp.dot(hb, wq_ref[...], preferred_element_type=F32)
    q = _bdot(_rms(q_lat, gq_ref[...]), wuq_ref[...])
    for hd in range(MLA_HEADS):
        sl = slice(hd * HEAD_SLAB, (hd + 1) * HEAD_SLAB)
        q_ref[:, sl] = _rope_slab(q[:, sl], cos, slo, shi).astype(BF16)

    ckv = _rms(jnp.dot(hb, wckv_ref[...], preferred_element_type=F32), gkv_ref[...])
    ckv_ref[...] = ckv
    cb = ckv.astype(BF16)
    kpe = _rope_slab(jnp.dot(hb, wkpe_ref[...], preferred_element_type=F32), cos, slo, shi)
    kpe_ref[...] = kpe[:, ROPE_LANE0:ROPE_LANE0 + QK_ROPE]
    kn = jnp.dot(cb, wukn_ref[...], preferred_element_type=F32)
    for hd in range(MLA_HEADS):
        sl = slice(hd * HEAD_SLAB, (hd + 1) * HEAD_SLAB)
        k_ref[:, sl] = (kn[:, sl] + kpe).astype(BF16)
    vt_ref[...] = _bdot_nt(wuvt_ref[...], cb).astype(BF16)


def _inproj(x, sh, sc, g_pre, rope_tabs, lw, rows_per_group):
    t, d = x.shape
    tm = min(ROW_TILE, rows_per_group)
    tpg = rows_per_group // tm
    cos, slo, shi = rope_tabs
    tab_spec = pl.BlockSpec((tm, LANES), lambda i: (i % tpg, 0))
    row = lambda n: pl.BlockSpec((tm, n), lambda i: (i, 0))
    weights = [lw["w_q"], lw["w_ckv"], lw["w_kpe"], lw["w_rw"], lw["w_gate"], lw["g_q"],
               lw["w_uq"], lw["g_kv"], lw["w_ukn"], lw["w_uv_t"]]
    slab = MLA_HEADS * HEAD_SLAB
    vt_spec = pl.BlockSpec((None, MLA_WIDTH, tm), lambda i: (i // tpg, 0, i % tpg))
    return pl.pallas_call(
        _inproj_kernel,
        grid=(t // tm,),
        in_specs=[row(d), _mod_spec(sh, tm, tpg), _mod_spec(sc, tm, tpg), _const_spec(g_pre.shape),
                  tab_spec, tab_spec, tab_spec] + [_const_spec(w.shape) for w in weights],
        out_specs=[row(slab), row(slab), vt_spec, row(KV_RANK), row(QK_ROPE),
                   row(RWKV_PROJ), row(2 * d)],
        out_shape=[jax.ShapeDtypeStruct((t, slab), BF16), jax.ShapeDtypeStruct((t, slab), BF16),
                   jax.ShapeDtypeStruct((t // rows_per_group, MLA_WIDTH, rows_per_group), BF16),
                   jax.ShapeDtypeStruct((t, KV_RANK), F32),
                   jax.ShapeDtypeStruct((t, QK_ROPE), F32), jax.ShapeDtypeStruct((t, RWKV_PROJ), F32),
                   jax.ShapeDtypeStruct((t, 2 * d), F32)],
        compiler_params=_params("arbitrary"),
        name="inproj",
    )(x, sh, sc, g_pre, cos, slo, shi, *weights)


def _attn_block(q_ref, k_ref, vt_ref, m_sc, l_sc, acc_sc, valid):
    for hd in range(MLA_HEADS):
        sl = slice(hd * HEAD_SLAB, (hd + 1) * HEAD_SLAB)
        rows = slice(hd * V_HEAD, (hd + 1) * V_HEAD)
        s = _bdot_nt(k_ref[:, sl], q_ref[:, sl]) * ATTN_SCALE
        if valid is not None:
            s = jnp.where(valid, s, -jnp.inf)
        m_prev = m_sc[hd]
        m_new = jnp.maximum(m_prev, jnp.max(s, axis=0, keepdims=True))
        alpha = jnp.exp(m_prev - m_new)
        p = jnp.exp(s - m_new)
        l_sc[hd] = alpha * l_sc[hd] + jnp.sum(p, axis=0, keepdims=True)
        m_sc[hd] = m_new
        acc_sc[rows, :] = alpha * acc_sc[rows, :] + jnp.dot(vt_ref[rows, :], p.astype(BF16),
                                                           preferred_element_type=F32)


def _attn_kernel(q_ref, k_ref, vt_ref, o_ref, m_sc, l_sc, acc_sc, *, tile):
    i = pl.program_id(1)
    j = pl.program_id(2)

    @pl.when(j == 0)
    def _():
        m_sc[...] = jnp.full(m_sc.shape, -jnp.inf, F32)
        l_sc[...] = jnp.zeros(l_sc.shape, F32)
        acc_sc[...] = jnp.zeros(acc_sc.shape, F32)

    @pl.when(j < i)
    def _():
        _attn_block(q_ref, k_ref, vt_ref, m_sc, l_sc, acc_sc, None)

    @pl.when(j == i)
    def _():
        key = lax.broadcasted_iota(jnp.int32, (tile, tile), 0)
        qry = lax.broadcasted_iota(jnp.int32, (tile, tile), 1)
        _attn_block(q_ref, k_ref, vt_ref, m_sc, l_sc, acc_sc, key <= qry)
        for hd in range(MLA_HEADS):
            rows = slice(hd * V_HEAD, (hd + 1) * V_HEAD)
            acc_sc[rows, :] = acc_sc[rows, :] * (1.0 / l_sc[hd])
        o_ref[...] = acc_sc[...].T.astype(o_ref.dtype)


def _prompt_attention(q, k, vt, batch, seq):
    tile = min(ATTN_TILE, seq)
    nt = seq // tile
    slab = MLA_HEADS * HEAD_SLAB
    q3, k3 = (a.reshape(batch, seq, slab) for a in (q, k))
    out = pl.pallas_call(
        functools.partial(_attn_kernel, tile=tile),
        grid=(batch, nt, nt),
        in_specs=[pl.BlockSpec((None, tile, slab), lambda b, i, j: (b, i, 0)),
                  pl.BlockSpec((None, tile, slab), lambda b, i, j: (b, jnp.minimum(i, j), 0)),
                  pl.BlockSpec((None, MLA_WIDTH, tile), lambda b, i, j: (b, 0, jnp.minimum(i, j)))],
        out_specs=pl.BlockSpec((None, tile, MLA_WIDTH), lambda b, i, j: (b, i, 0)),
        out_shape=jax.ShapeDtypeStruct((batch, seq, MLA_WIDTH), BF16),
        scratch_shapes=[pltpu.VMEM((MLA_HEADS, 1, tile), F32), pltpu.VMEM((MLA_HEADS, 1, tile), F32),
                        pltpu.VMEM((MLA_WIDTH, tile), F32)],
        compiler_params=_params("arbitrary", "arbitrary", "arbitrary"),
        name="prompt_attention",
    )(q3, k3, vt)
    return out.reshape(batch * seq, MLA_WIDTH)


def _qlat_kernel(q_ref, wuk_ref, o_ref):
    for hd in range(MLA_HEADS):
        o_ref[hd] = jnp.dot(q_ref[:, hd * HEAD_SLAB:(hd + 1) * HEAD_SLAB], wuk_ref[hd],
                            preferred_element_type=F32)


def _decode_kernel(pt_ref, qlat_ref, qpe_ref, cnew_ref, knew_ref, ckv_hbm, kpet_hbm, o_ref,
                   ckv_buf, kpet_buf, sems, *, layer, n_pages, page):
    b = pl.program_id(0)
    nb = pl.num_programs(0)

    def page_copies(seq, slot, pg):
        pid = pt_ref[seq * n_pages + pg]
        rows = pl.ds(pl.multiple_of(pg * page, page), page)
        return (pltpu.make_async_copy(ckv_hbm.at[layer, pid], ckv_buf.at[slot, rows, :], sems.at[0, slot]),
                pltpu.make_async_copy(kpet_hbm.at[layer, pid], kpet_buf.at[slot, :, rows], sems.at[1, slot]))

    def start_gather(seq, slot):
        def body(pg, carry):
            for cp in page_copies(seq, slot, pg):
                cp.start()
            return carry
        lax.fori_loop(0, n_pages, body, 0)

    def wait_gather(seq, slot):
        def body(pg, carry):
            for cp in page_copies(seq, slot, pg):
                cp.wait()
            return carry
        lax.fori_loop(0, n_pages, body, 0)

    slot = b % DECODE_SLOTS

    @pl.when(b == 0)
    def _():
        start_gather(b, slot)

    @pl.when(b + 1 < nb)
    def _():
        start_gather(b + 1, (b + 1) % DECODE_SLOTS)

    wait_gather(b, slot)

    ql, qp = qlat_ref[...], qpe_ref[...]
    cn, kn = cnew_ref[...], knew_ref[...]
    ck = ckv_buf[slot].astype(BF16)
    s = (_bdot_nt(ql, ck) + _bdot(qp, kpet_buf[slot])) * ATTN_SCALE
    s_cur = (jnp.sum(ql * cn, axis=-1, keepdims=True)
             + jnp.sum(qp * kn, axis=-1, keepdims=True)) * ATTN_SCALE
    m = jnp.maximum(jnp.max(s, axis=-1, keepdims=True), s_cur)
    p = jnp.exp(s - m)
    pc = jnp.exp(s_cur - m)
    denom = jnp.sum(p, axis=-1, keepdims=True) + pc
    o_ref[...] = (jnp.dot(p.astype(BF16), ck, preferred_element_type=F32) + pc * cn) / denom


def _olat_kernel(o_ref, wuv_ref, y_ref):
    acc = None
    for hd in range(MLA_HEADS):
        d = _bdot(o_ref[hd], wuv_ref[hd])
        acc = d if acc is None else acc + d
    y_ref[...] = acc.astype(y_ref.dtype)


def _sample_attention(layer, q_slab, ckv_new, kpe_new, cache_ckv, cache_kpe_t, page_table, lw):
    b = q_slab.shape[0]
    n_pages = page_table.shape[1]
    page = cache_ckv.shape[2]
    past_len = n_pages * page
    qlat = pl.pallas_call(
        _qlat_kernel,
        out_shape=jax.ShapeDtypeStruct((MLA_HEADS, b, KV_RANK), F32),
        compiler_params=pltpu.CompilerParams(vmem_limit_bytes=VMEM_LIMIT_BYTES),
        name="decode_qlat",
    )(q_slab, lw["w_uk_t"])
    qlat = qlat.transpose(1, 0, 2)
    qpe = q_slab.reshape(b, MLA_HEADS, HEAD_SLAB)[:, :, ROPE_LANE0:ROPE_LANE0 + QK_ROPE].astype(F32)

    per_seq = lambda shape: pl.BlockSpec((None,) + shape, lambda bi, pt: (bi, 0, 0))
    grid_spec = pltpu.PrefetchScalarGridSpec(
        num_scalar_prefetch=1,
        grid=(b,),
        in_specs=[per_seq((MLA_HEADS, KV_RANK)), per_seq((MLA_HEADS, QK_ROPE)),
                  per_seq((1, KV_RANK)), per_seq((1, QK_ROPE)),
                  pl.BlockSpec(memory_space=pl.ANY), pl.BlockSpec(memory_space=pl.ANY)],
        out_specs=per_seq((MLA_HEADS, KV_RANK)),
        scratch_shapes=[pltpu.VMEM((DECODE_SLOTS, past_len, KV_RANK), F32),
                        pltpu.VMEM((DECODE_SLOTS, QK_ROPE, past_len), F32),
                        pltpu.SemaphoreType.DMA((2, DECODE_SLOTS))],
    )
    olat = pl.pallas_call(
        functools.partial(_decode_kernel, layer=layer, n_pages=n_pages, page=page),
        grid_spec=grid_spec,
        out_shape=jax.ShapeDtypeStruct((b, MLA_HEADS, KV_RANK), F32),
        compiler_params=_params("arbitrary"),
        name="decode_attention",
    )(page_table.reshape(-1), qlat, qpe, ckv_new.reshape(b, 1, KV_RANK), kpe_new.reshape(b, 1, QK_ROPE),
      cache_ckv, cache_kpe_t)
    return pl.pallas_call(
        _olat_kernel,
        out_shape=jax.ShapeDtypeStruct((b, MLA_WIDTH), BF16),
        compiler_params=pltpu.CompilerParams(vmem_limit_bytes=VMEM_LIMIT_BYTES),
        name="decode_out",
    )(olat.transpose(1, 0, 2), lw["w_uv_wide"])


def _softplus(x):
    return jnp.maximum(x, 0.0) + jnp.log(1.0 + jnp.exp(-jnp.abs(x)))


def _rwkv_token_ops(zm, w0, w2, a0, a2, g2, k_k, k_a, r_k, seg):
    w = RWKV_WIDTH
    r, k, v = zm[:, :w], zm[:, w:2 * w], zm[:, 2 * w:3 * w]
    xwa = zm[:, 3 * w:3 * w + DECAY_LORA + AAA_LORA]
    xg = zm[:, 3 * w + DECAY_LORA + AAA_LORA:]
    w_log = -_softplus(-(w0 + _bdot(jnp.tanh(xwa), w2))) - 0.5
    a = jax.nn.sigmoid(a0 + _bdot(xwa, a2))
    g = _bdot(jax.nn.sigmoid(xg), g2)
    kk = k * k_k
    kk = kk / jnp.maximum(jnp.sqrt(_split_dot(kk * kk, seg, 2)), 1e-12)
    k2 = k * (1.0 + (a - 1.0) * k_a)
    bonus = _split_dot(r * k2 * r_k, seg, 2) * v
    return r, -jnp.exp(w_log), k2, v, -kk, kk * a, g, bonus


def _rwkv_consts(lw):
    return [lw["mu"], lw["rw_w0"], lw["rw_w2"], lw["rw_a0"], lw["rw_a2"], lw["rw_g2"], lw["rw_k_k"],
            lw["rw_k_a"], lw["rw_r_k"], lw["seg_ones"]]


def _token_prep_kernel(z_ref, zprev_ref, mu_ref, w0_ref, w2_ref, a0_ref, a2_ref, g2_ref, kk_ref, ka_ref,
                       rk_ref, seg_ref, *out_refs):
    z = z_ref[...]
    zm = z + mu_ref[...] * (zprev_ref[...] - z)
    outs = _rwkv_token_ops(zm, w0_ref[...], w2_ref[...], a0_ref[...], a2_ref[...], g2_ref[...],
                           kk_ref[...], ka_ref[...], rk_ref[...], seg_ref[...])
    for o_ref, val in zip(out_refs, outs):
        o_ref[...] = val


def _token_prep(z_rw, z_prev, lw):
    t = z_rw.shape[0]
    tm = min(ROW_TILE, t)
    row = lambda n: pl.BlockSpec((tm, n), lambda i: (i, 0))
    consts = _rwkv_consts(lw)
    return pl.pallas_call(
        _token_prep_kernel,
        grid=(t // tm,),
        in_specs=[row(RWKV_PROJ), row(RWKV_PROJ)] + [_const_spec(c.shape) for c in consts],
        out_specs=[row(RWKV_WIDTH)] * 8,
        out_shape=[jax.ShapeDtypeStruct((t, RWKV_WIDTH), F32)] * 8,
        compiler_params=_params("arbitrary"),
        name="rwkv_token_prep",
    )(z_rw, z_prev, *consts)


def _pair_norm_gate(y, seg_pair, lnw, lnb, bonus, g):
    inv_n = 1.0 / RWKV_HEAD
    cen = y - _split_dot(y, seg_pair, 2) * inv_n
    var = _split_dot(cen * cen, seg_pair, 2) * inv_n
    return (cen * lax.rsqrt(var + LNX_EPS) * lnw + lnb + bonus) * g


def _chunk_prep_kernel(z_ref, s0_ref, mu_ref, w0_ref, w2_ref, a0_ref, a2_ref, g2_ref, kk_ref, ka_ref,
                       rk_ref, seg_ref,
                       w_ref, r_ref, bh_ref, kh_ref, v_ref, mrb_ref, u0_ref, y0_ref, g_ref, bonus_ref,
                       elast_ref, last_ref, carry_sc, *, chunk):
    i = pl.program_id(1)
    z = z_ref[...]
    rows = z.shape[0]
    first = jnp.where(i == 0, s0_ref[...], carry_sc[...])
    is_row0 = lax.broadcasted_iota(jnp.int32, z.shape, 0) == 0
    z_prev = jnp.where(is_row0, first, pltpu.roll(z, 1, 0))
    carry_sc[...] = z[rows - 1:rows, :]
    last_ref[...] = z[rows - 1:rows, :]
    zm = z + mu_ref[...] * (z_prev - z)
    r, lw, k2, v, an, bn, g, bonus = _rwkv_token_ops(
        zm, w0_ref[...], w2_ref[...], a0_ref[...], a2_ref[...], g2_ref[...], kk_ref[...], ka_ref[...],
        rk_ref[...], seg_ref[...])
    g_ref[...] = g
    bonus_ref[...] = bonus
    v_ref[...] = v.astype(BF16)

    ri = lax.broadcasted_iota(jnp.int32, (rows, rows), 0)
    ci = lax.broadcasted_iota(jnp.int32, (rows, rows), 1)
    chunk_start = ri - jnp.bitwise_and(ri, chunk - 1)
    tri = jnp.where((ci <= ri) & (ci >= chunk_start), 1.0, 0.0).astype(BF16)
    cum = _ones_split_dot(tri, lw, 3)
    n_chunks = rows // chunk
    lasts = [cum[(c + 1) * chunk - 1:(c + 1) * chunk, :] for c in range(n_chunks)]
    cum_last = jnp.concatenate([jnp.broadcast_to(x, (chunk, RWKV_WIDTH)) for x in lasts], axis=0)
    for c in range(n_chunks):
        elast_ref[c] = jnp.broadcast_to(jnp.exp(lasts[c]), (SUBLANES, RWKV_WIDTH))
    e_neg = jnp.exp(-cum)
    e_tail = jnp.exp(cum_last - cum)
    a_t = an * jnp.exp(cum - lw)
    r_t = r * jnp.exp(cum)
    b_t, k_t = bn * e_neg, k2 * e_neg
    r_ref[...] = r_t.astype(BF16)
    bh_ref[...] = (bn * e_tail).astype(BF16)
    kh_ref[...] = (k2 * e_tail).astype(BF16)

    lane = lax.broadcasted_iota(jnp.int32, (chunk, LANES), 1)
    tok = lax.broadcasted_iota(jnp.int32, (chunk, LANES), 0)
    left = lane < RWKV_HEAD
    col = jnp.bitwise_and(lane, RWKV_HEAD - 1)
    strict, incl = col < tok, col <= tok
    zero = jnp.zeros((chunk, LANES), F32)
    insts = [(slice(c * chunk, (c + 1) * chunk), slice(p * LANES, (p + 1) * LANES))
             for c in range(n_chunks) for p in range(RWKV_PAIRS)]

    prods = []
    for rs, ls in insts:
        ar = jnp.concatenate([a_t[rs, ls], r_t[rs, ls]], axis=0)
        bp, kp = b_t[rs, ls], k_t[rs, ls]
        bk = jnp.concatenate([jnp.where(left, bp, zero), jnp.where(left, zero, bp),
                              jnp.where(left, kp, zero), jnp.where(left, zero, kp)], axis=0)
        prods.append(_bdot_nt(ar, bk))
    l_pows, sols = [], []
    for (rs, ls), pq in zip(insts, prods):
        ab, ak = pq[:chunk, :LANES], pq[:chunk, LANES:]
        rb, rk = pq[chunk:, :LANES], pq[chunk:, LANES:]
        mrb_ref[rs, ls] = jnp.where(incl, rb, 0.0).astype(BF16)
        vp = v[rs, ls]
        v_bd = jnp.concatenate([jnp.where(left, vp, zero), jnp.where(left, zero, vp)], axis=0)
        lm = jnp.concatenate([jnp.where(strict, ak, 0.0), jnp.where(incl, rk, 0.0)], axis=0)
        lmv = _bdot(lm, v_bd)
        y0_ref[rs, ls] = lmv[chunk:]
        lv = lmv[:chunk]
        ap = a_t[rs, ls]
        sols.append(jnp.concatenate([jnp.where(left, ap, pltpu.roll(lv, RWKV_HEAD, 1)),
                                     jnp.where(left, pltpu.roll(ap, RWKV_HEAD, 1), lv)], axis=0))
        l_pows.append(jnp.concatenate([jnp.where(strict & left, ab, 0.0),
                                       jnp.where(strict & jnp.logical_not(left), ab, 0.0)], axis=0))
    span = 1
    while span < chunk:
        sols = [x + _bdot(lp, x) for lp, x in zip(l_pows, sols)]
        span *= 2
        if span < chunk:
            l_pows = [_bdot(lp, lp) for lp in l_pows]
    for (rs, ls), x in zip(insts, sols):
        top, bot = x[:chunk], x[chunk:]
        w_ref[rs, ls] = jnp.where(left, top, pltpu.roll(bot, RWKV_HEAD, 1)).astype(BF16)
        u0_ref[rs, ls] = jnp.where(left, pltpu.roll(top, RWKV_HEAD, 1), bot)


def _chunk_prep(z_rw, shift0, lw, batch, seq):
    chunk = min(SCAN_CHUNK, seq)
    tm = min(ROW_TILE, seq)
    tiles = seq // tm
    cpt = tm // chunk
    n_chunks = seq // chunk
    row = lambda n: pl.BlockSpec((tm, n), lambda b, i: (b * tiles + i, 0))
    per_seq = pl.BlockSpec((None, 1, RWKV_PROJ), lambda b, i: (b, 0, 0))
    consts = _rwkv_consts(lw)
    t = batch * seq
    wide = lambda dt: jax.ShapeDtypeStruct((t, RWKV_WIDTH), dt)
    outs = pl.pallas_call(
        functools.partial(_chunk_prep_kernel, chunk=chunk),
        grid=(batch, tiles),
        in_specs=[row(RWKV_PROJ), per_seq] + [_const_spec(c.shape) for c in consts],
        out_specs=[row(RWKV_WIDTH)] * 10
                  + [pl.BlockSpec((None, cpt, SUBLANES, RWKV_WIDTH), lambda b, i: (b, i, 0, 0)), per_seq],
        out_shape=[wide(BF16)] * 6 + [wide(F32)] * 4
                  + [jax.ShapeDtypeStruct((batch, n_chunks, SUBLANES, RWKV_WIDTH), F32),
                     jax.ShapeDtypeStruct((batch, 1, RWKV_PROJ), F32)],
        scratch_shapes=[pltpu.VMEM((1, RWKV_PROJ), F32)],
        compiler_params=_params("arbitrary", "arbitrary"),
        name="rwkv_chunk_prep",
    )(z_rw, shift0.reshape(batch, 1, RWKV_PROJ), *consts)
    return outs[:10], outs[10], outs[11]


def _chunk_scan_kernel(w_ref, r_ref, bh_ref, kh_ref, v_ref, mrb_ref, u0_ref, y0_ref, g_ref, bonus_ref,
                       elast_ref, s0_ref, lnw_ref, lnb_ref, seg_ref, y_ref, sout_ref, state_sc, *, chunk):
    c = pl.program_id(0)
    nb = w_ref.shape[0]
    hh = RWKV_HEAD
    insts = [(b, p, slice(p * LANES, (p + 1) * LANES)) for b in range(nb) for p in range(RWKV_PAIRS)]

    @pl.when(c == 0)
    def _():
        state_sc[...] = jnp.zeros(state_sc.shape, F32)
        for b, p, _ in insts:
            state_sc[b, p, :hh, :hh] = s0_ref[b, 2 * p]
            state_sc[b, p, hh:, hh:] = s0_ref[b, 2 * p + 1]

    lane = lax.broadcasted_iota(jnp.int32, (chunk, LANES), 1)
    left = lane < hh
    zero = jnp.zeros((chunk, LANES), F32)
    vi = lax.broadcasted_iota(jnp.int32, (LANES, LANES), 0)
    ki = lax.broadcasted_iota(jnp.int32, (LANES, LANES), 1)
    same_head = (vi < hh) == (ki < hh)
    seg = seg_ref[...]

    ps = [_bdot_nt(jnp.concatenate([w_ref[b, :, ls], r_ref[b, :, ls]], axis=0), state_sc[b, p])
          for b, p, ls in insts]
    ys = []
    for (b, p, ls), wr_s in zip(insts, ps):
        u = wr_s[:chunk] + u0_ref[b, :, ls]
        u_bd = jnp.concatenate([jnp.where(left, u, zero), jnp.where(left, zero, u)], axis=0)
        ys.append(wr_s[chunk:] + _bdot(mrb_ref[b, :, ls], u_bd) + y0_ref[b, :, ls])
        uv = jnp.concatenate([u.astype(BF16), v_ref[b, :, ls]], axis=0)
        bk = jnp.concatenate([bh_ref[b, :, ls], kh_ref[b, :, ls]], axis=0)
        upd = _bdot_tn(uv, bk)
        state_sc[b, p] = state_sc[b, p] * elast_ref[b, 0:1, ls] + jnp.where(same_head, upd, 0.0)
    for (b, p, ls), y in zip(insts, ys):
        y_ref[b, :, ls] = _pair_norm_gate(y, seg, lnw_ref[:, ls], lnb_ref[:, ls], bonus_ref[b, :, ls],
                                          g_ref[b, :, ls]).astype(y_ref.dtype)

    @pl.when(c == pl.num_programs(0) - 1)
    def _():
        for b, p, _ in insts:
            sout_ref[b, 2 * p] = state_sc[b, p, :hh, :hh]
            sout_ref[b, 2 * p + 1] = state_sc[b, p, hh:, hh:]


def _chunk_scan(chunk_ops, elast, state0, lw, batch, seq):
    chunk = min(SCAN_CHUNK, seq)
    n_chunks = seq // chunk
    tok = pl.BlockSpec((batch, chunk, RWKV_WIDTH), lambda c: (0, c, 0))
    ops3 = [a.reshape(batch, seq, RWKV_WIDTH) for a in chunk_ops]
    y, s_new = pl.pallas_call(
        functools.partial(_chunk_scan_kernel, chunk=chunk),
        grid=(n_chunks,),
        in_specs=[tok] * 10 + [pl.BlockSpec((batch, SUBLANES, RWKV_WIDTH), lambda c: (0, c, 0)),
                               _const_spec(state0.shape), _const_spec(lw["lnx_w"].shape),
                               _const_spec(lw["lnx_b"].shape), _const_spec(lw["seg_pair"].shape)],
        out_specs=[tok, _const_spec(state0.shape)],
        out_shape=[jax.ShapeDtypeStruct((batch, seq, RWKV_WIDTH), BF16),
                   jax.ShapeDtypeStruct(state0.shape, F32)],
        scratch_shapes=[pltpu.VMEM((batch, RWKV_PAIRS, LANES, LANES), F32)],
        compiler_params=_params("arbitrary"),
        name="rwkv_chunk_scan",
    )(*ops3, elast.reshape(batch, n_chunks * SUBLANES, RWKV_WIDTH), state0, lw["lnx_w"], lw["lnx_b"],
      lw["seg_pair"])
    return y.reshape(batch * seq, RWKV_WIDTH), s_new


def _wkv_step_kernel(r_ref, lw_ref, k_ref, v_ref, an_ref, bn_ref, g_ref, bonus_ref, s_ref, lnw_ref, lnb_ref,
                     y_ref, sout_ref):
    n = RWKV_HEAD
    eye = (lax.broadcasted_iota(jnp.int32, (n, n), 0) == lax.broadcasted_iota(jnp.int32, (n, n), 1))
    for sq in range(s_ref.shape[0]):
        for hd in range(RWKV_HEADS):
            sl = slice(hd * n, (hd + 1) * n)
            state = s_ref[sq, hd]
            sa = jnp.sum(state * an_ref[sq, :, sl], axis=-1, keepdims=True)
            v_col = jnp.sum(jnp.where(eye, v_ref[sq, :, sl], 0.0), axis=-1, keepdims=True)
            new = state * jnp.exp(lw_ref[sq, :, sl]) + sa * bn_ref[sq, :, sl] + v_col * k_ref[sq, :, sl]
            sout_ref[sq, hd] = new
            y_col = jnp.sum(new * r_ref[sq, :, sl], axis=-1, keepdims=True)
            y = jnp.sum(jnp.where(eye, y_col, 0.0), axis=0, keepdims=True)
            mean = jnp.mean(y, axis=-1, keepdims=True)
            cen = y - mean
            var = jnp.mean(cen * cen, axis=-1, keepdims=True)
            yn = cen * lax.rsqrt(var + LNX_EPS) * lnw_ref[:, sl] + lnb_ref[:, sl]
            y_ref[sq, :, sl] = (yn + bonus_ref[sq, :, sl]) * g_ref[sq, :, sl]


def _wkv_step(prep, state, lw):
    b = state.shape[0]
    per = min(WKV_STEP_SEQS, b)
    tok = pl.BlockSpec((per, 1, RWKV_WIDTH), lambda i: (i, 0, 0))
    st = pl.BlockSpec((per, RWKV_HEADS, RWKV_HEAD, RWKV_HEAD), lambda i: (i, 0, 0, 0))
    y, s_new = pl.pallas_call(
        _wkv_step_kernel,
        grid=(b // per,),
        in_specs=[tok] * 8 + [st, _const_spec(lw["lnx_w"].shape), _const_spec(lw["lnx_b"].shape)],
        out_specs=[tok, st],
        out_shape=[jax.ShapeDtypeStruct((b, 1, RWKV_WIDTH), F32), jax.ShapeDtypeStruct(state.shape, F32)],
        compiler_params=_params("arbitrary"),
        name="wkv_step",
    )(*[p.reshape(b, 1, RWKV_WIDTH) for p in prep], state, lw["lnx_w"], lw["lnx_b"])
    return y.reshape(b, RWKV_WIDTH), s_new


def _merge_kernel(x_ref, ym_ref, yr_ref, gate_ref, gt_ref, gpost_ref, wbm_ref, wbr_ref, wout_ref, o_ref):
    d = x_ref.shape[-1]
    merged = (gate_ref[:, :d] * _bdot(ym_ref[...], wbm_ref[...])
              + gate_ref[:, d:] * _bdot(yr_ref[...], wbr_ref[...]))
    out = _bdot(merged, wout_ref[...])
    o_ref[...] = x_ref[...] + gt_ref[...] * _rms(out, gpost_ref[...])


def _merge(x, y_mla, y_rw, gate, gt, g_post, lw, rows_per_group):
    t, d = x.shape
    tm = min(ROW_TILE, rows_per_group)
    tpg = rows_per_group // tm
    row = lambda n: pl.BlockSpec((tm, n), lambda i: (i, 0))
    weights = [lw["w_branch_mla"], lw["w_branch_rwkv"], lw["w_out"]]
    return pl.pallas_call(
        _merge_kernel,
        grid=(t // tm,),
        in_specs=[row(d), row(MLA_WIDTH), row(RWKV_WIDTH), row(2 * d), _mod_spec(gt, tm, tpg),
                  _const_spec(g_post.shape)] + [_const_spec(w.shape) for w in weights],
        out_specs=row(d),
        out_shape=jax.ShapeDtypeStruct((t, d), F32),
        compiler_params=_params("arbitrary"),
        name="merge",
    )(x, y_mla, y_rw, gate, gt, g_post, *weights)


def _ffn_kernel(x_ref, sh_ref, sc_ref, gt_ref, gpre_ref, gpost_ref, w1_ref, w2_ref, o_ref):
    x = x_ref[...]
    hb = (_rms(x, gpre_ref[...]) * (1.0 + sc_ref[...]) + sh_ref[...]).astype(BF16)
    acc = jnp.zeros(x.shape, F32)
    for c0 in range(0, w1_ref.shape[1], FFN_CHUNK):
        a = jnp.maximum(jnp.dot(hb, w1_ref[:, c0:c0 + FFN_CHUNK], preferred_element_type=F32), 0.0)
        acc = acc + _bdot(a * a, w2_ref[c0:c0 + FFN_CHUNK, :])
    o_ref[...] = x + gt_ref[...] * _rms(acc, gpost_ref[...])


def _ffn(x, sh, sc, gt, g_pre, g_post, lw, rows_per_group):
    t, d = x.shape
    tm = min(FFN_ROW_TILE, rows_per_group)
    tpg = rows_per_group // tm
    row = pl.BlockSpec((tm, d), lambda i: (i, 0))
    return pl.pallas_call(
        _ffn_kernel,
        grid=(t // tm,),
        in_specs=[row, _mod_spec(sh, tm, tpg), _mod_spec(sc, tm, tpg), _mod_spec(gt, tm, tpg),
                  _const_spec(g_pre.shape), _const_spec(g_post.shape),
                  _const_spec(lw["w_ff1"].shape), _const_spec(lw["w_ff2"].shape)],
        out_specs=row,
        out_shape=jax.ShapeDtypeStruct((t, d), F32),
        compiler_params=_params("arbitrary"),
        name="ffn",
    )(x, sh, sc, gt, g_pre, g_post, lw["w_ff1"], lw["w_ff2"])


def _rope_tables(pos):
    inv = ROPE_BASE ** (-jnp.arange(HALF_ROPE, dtype=F32) / HALF_ROPE)
    ang = pos.astype(F32)[:, None] * inv[None, :]
    cos, sin = jnp.cos(ang), jnp.sin(ang)
    n = pos.shape[0]
    ones = jnp.ones((n, ROPE_LANE0), F32)
    zeros = jnp.zeros((n, ROPE_LANE0), F32)
    pad = jnp.zeros((n, LANES - ROPE_LANE0 - QK_ROPE), F32)
    z16 = jnp.zeros((n, HALF_ROPE), F32)
    cos_t = jnp.concatenate([ones, cos, cos, pad], axis=1)
    sin_lo = jnp.concatenate([zeros, -sin, z16, pad], axis=1)
    sin_hi = jnp.concatenate([zeros, z16, sin, pad], axis=1)
    return cos_t, sin_lo, sin_hi


def _layer_weights(l, p):
    w_in = p["w_in"][l]
    w_q = w_in[:, :Q_RANK]
    w_ckv = w_in[:, Q_RANK:Q_RANK + KV_RANK]
    w_kpe = jnp.pad(w_in[:, Q_RANK + KV_RANK:MLA_PROJ], ((0, 0), (ROPE_LANE0, LANES - ROPE_LANE0 - QK_ROPE)))
    w_rw = w_in[:, MLA_PROJ:MLA_PROJ + RWKV_PROJ]
    w_gate = w_in[:, MLA_PROJ + RWKV_PROJ:]
    qk = QK_NOPE + QK_ROPE
    w_uq = jnp.pad(p["w_uq"][l].reshape(Q_RANK, MLA_HEADS, qk), ((0, 0), (0, 0), (0, HEAD_SLAB - qk)))
    w_ukv = p["w_ukv"][l].reshape(KV_RANK, MLA_HEADS, QK_NOPE + V_HEAD)
    w_uk, w_uv = w_ukv[..., :QK_NOPE], w_ukv[..., QK_NOPE:]
    w_ukn = jnp.pad(w_uk, ((0, 0), (0, 0), (0, HEAD_SLAB - QK_NOPE)))
    w_uk_t = jnp.pad(w_uk.transpose(1, 2, 0), ((0, 0), (0, HEAD_SLAB - QK_NOPE), (0, 0)))
    eye = jnp.eye(MLA_HEADS, dtype=F32)
    w_uv_wide = (w_uv.transpose(1, 0, 2)[:, :, None, :] * eye[:, None, :, None]).reshape(
        MLA_HEADS, KV_RANK, MLA_WIDTH)
    lora_pad_lo = ((0, AAA_LORA), (0, 0))
    lora_pad_hi = ((DECAY_LORA, 0), (0, 0))
    heads = jnp.arange(RWKV_WIDTH) // RWKV_HEAD
    seg_ones = (heads[:, None] == heads[None, :]).astype(BF16)
    row2 = lambda a: a.reshape(1, -1)
    return {
        "w_q": w_q.astype(BF16), "w_ckv": w_ckv.astype(BF16), "w_kpe": w_kpe.astype(BF16),
        "w_rw": w_rw.astype(BF16), "w_gate": w_gate.astype(BF16),
        "g_q": row2(p["g_q_norm"][l]), "g_kv": row2(p["g_kv_norm"][l]),
        "w_uq": w_uq.reshape(Q_RANK, MLA_HEADS * HEAD_SLAB).astype(BF16),
        "w_ukn": w_ukn.reshape(KV_RANK, MLA_HEADS * HEAD_SLAB).astype(BF16),
        "w_uv_t": w_uv.reshape(KV_RANK, MLA_WIDTH).T.astype(BF16),
        "w_uk_t": w_uk_t.astype(BF16), "w_uv_wide": w_uv_wide.astype(BF16),
        "mu": row2(p["mu_shift"][l]),
        "rw_w0": row2(p["rw_w0"][l]), "rw_a0": row2(p["rw_a0"][l]),
        "rw_w2": jnp.pad(p["rw_w2"][l], lora_pad_lo).astype(BF16),
        "rw_a2": jnp.pad(p["rw_a2"][l], lora_pad_hi).astype(BF16),
        "rw_g2": p["rw_g2"][l].astype(BF16),
        "rw_k_k": row2(p["rw_k_k"][l]), "rw_k_a": row2(p["rw_k_a"][l]), "rw_r_k": row2(p["rw_r_k"][l]),
        "seg_ones": seg_ones, "seg_pair": seg_ones[:LANES, :LANES],
        "lnx_w": row2(p["rw_lnx_w"][l]), "lnx_b": row2(p["rw_lnx_b"][l]),
        "w_branch_mla": p["w_branch_mla"][l].astype(BF16), "w_branch_rwkv": p["w_branch_rwkv"][l].astype(BF16),
        "w_out": p["w_out"][l].astype(BF16),
        "w_ff1": p["w_ff1"][l].astype(BF16), "w_ff2": p["w_ff2"][l].astype(BF16),
        "g_pre_mix": row2(p["g_pre_mix"][l]), "g_post_mix": row2(p["g_post_mix"][l]),
        "g_pre_ff": row2(p["g_pre_ff"][l]), "g_post_ff": row2(p["g_post_ff"][l]),
    }


def _trunk(x, mods, rope_tabs, rows_per_group, layer_weights, mixers):
    ckv_rows, kpe_rows, wkv_states, shift_rows = [], [], [], []
    for l in range(DEPTH):
        lw = layer_weights[l]
        sh1, sc1, gt1, sh2, sc2, gt2 = mods[l]
        q, k, vt, ckv, kpe, z_rw, gate = _inproj(x, sh1, sc1, lw["g_pre_mix"], rope_tabs, lw, rows_per_group)
        y_mla, y_rw, wkv_new, shift_new = mixers(l, lw, q, k, vt, ckv, kpe, z_rw)
        x = _merge(x, y_mla, y_rw, gate, gt1, lw["g_post_mix"], lw, rows_per_group)
        x = _ffn(x, sh2, sc2, gt2, lw["g_pre_ff"], lw["g_post_ff"], lw, rows_per_group)
        ckv_rows.append(ckv)
        kpe_rows.append(kpe)
        wkv_states.append(wkv_new)
        shift_rows.append(shift_new)
    return x, jnp.stack(ckv_rows), jnp.stack(kpe_rows), jnp.stack(wkv_states), jnp.stack(shift_rows)


def kernel(x_prompt, x_sample, c_prompt, c_sample, cache_ckv, cache_kpe, state_wkv, state_shift, page_table, w_ada, b_ada, g_pre_mix, g_post_mix, g_pre_ff, g_post_ff, w_in, mu_shift, g_q_norm, w_uq, g_kv_norm, w_ukv, rw_w0, rw_w2, rw_a0, rw_a2, rw_g2, rw_k_k, rw_k_a, rw_r_k, rw_lnx_w, rw_lnx_b, w_branch_mla, w_branch_rwkv, w_out, w_ff1, w_ff2):
    params = dict(g_pre_mix=g_pre_mix, g_post_mix=g_post_mix, g_pre_ff=g_pre_ff, g_post_ff=g_post_ff,
                  w_in=w_in, mu_shift=mu_shift, g_q_norm=g_q_norm, w_uq=w_uq, g_kv_norm=g_kv_norm,
                  w_ukv=w_ukv, rw_w0=rw_w0, rw_w2=rw_w2, rw_a0=rw_a0, rw_a2=rw_a2, rw_g2=rw_g2,
                  rw_k_k=rw_k_k, rw_k_a=rw_k_a, rw_r_k=rw_r_k, rw_lnx_w=rw_lnx_w, rw_lnx_b=rw_lnx_b,
                  w_branch_mla=w_branch_mla, w_branch_rwkv=w_branch_rwkv, w_out=w_out,
                  w_ff1=w_ff1, w_ff2=w_ff2)
    b_p, s_p, d = x_prompt.shape
    b_s, s_s, _ = x_sample.shape
    assert s_s == 1, "the sample group advances one token per sequence"
    past_len = page_table.shape[1] * cache_ckv.shape[2]
    layer_weights = [_layer_weights(l, params) for l in range(DEPTH)]

    mod = _adaln(jnp.concatenate([c_prompt, c_sample], axis=0), w_ada, b_ada)
    mod = mod.reshape(DEPTH, b_p + b_s, 6, d)
    mods_p = [[mod[l, :b_p, j].reshape(b_p, 1, d) for j in range(6)] for l in range(DEPTH)]
    mods_s = [[mod[l, b_p:, j].reshape(1, b_s, d) for j in range(6)] for l in range(DEPTH)]

    shift0_p = jnp.zeros((b_p, RWKV_PROJ), F32)
    wkv0_p = jnp.zeros((b_p, RWKV_HEADS, RWKV_HEAD, RWKV_HEAD), F32)

    def prompt_mixers(l, lw, q, k, vt, ckv, kpe, z_rw):
        y_mla = _prompt_attention(q, k, vt, b_p, s_p)
        chunk_ops, elast, last = _chunk_prep(z_rw, shift0_p, lw, b_p, s_p)
        y_rw, wkv_new = _chunk_scan(chunk_ops, elast, wkv0_p, lw, b_p, s_p)
        return y_mla, y_rw, wkv_new, last.reshape(b_p, RWKV_PROJ)

    y_p, p_ckv, p_kpe, p_wkv, p_shift = _trunk(
        x_prompt.reshape(b_p * s_p, d), mods_p, _rope_tables(jnp.arange(s_p)), s_p, layer_weights,
        prompt_mixers)

    cache_kpe_t = cache_kpe.transpose(0, 1, 3, 2)

    def sample_mixers(l, lw, q, k, vt, ckv, kpe, z_rw):
        y_mla = _sample_attention(l, q, ckv, kpe, cache_ckv, cache_kpe_t, page_table, lw)
        prep = _token_prep(z_rw, state_shift[l], lw)
        y_rw, wkv_new = _wkv_step(prep, state_wkv[l], lw)
        return y_mla, y_rw, wkv_new, z_rw

    y_s, s_ckv, s_kpe, s_wkv, s_shift = _trunk(
        x_sample.reshape(b_s, d), mods_s, _rope_tables(jnp.full((b_s,), past_len)), b_s, layer_weights,
        sample_mixers)

    return (y_p.reshape(b_p, s_p, d), y_s.reshape(b_s, 1, d),
            p_ckv.reshape(DEPTH, b_p, s_p, KV_RANK), p_kpe.reshape(DEPTH, b_p, s_p, QK_ROPE), p_wkv, p_shift,
            s_ckv.reshape(DEPTH, b_s, 1, KV_RANK), s_kpe.reshape(DEPTH, b_s, 1, QK_ROPE), s_wkv, s_shift)
```

```python
import functools

import jax
import jax.numpy as jnp
from jax import lax
from jax.experimental import pallas as pl
from jax.experimental.pallas import tpu as pltpu

F32 = jnp.float32
BF16 = jnp.bfloat16

D_MODEL = 1024
DEPTH = 4
MLA_HEADS = 8
QK_NOPE = 64
QK_ROPE = 32
V_HEAD = 64
Q_RANK = 384
KV_RANK = 256
MLA_WIDTH = MLA_HEADS * V_HEAD
MLA_PROJ = Q_RANK + KV_RANK + QK_ROPE
ATTN_SCALE = (QK_NOPE + QK_ROPE) ** -0.5
LOG2_E = 1.4426950408889634
ROPE_BASE = 10000.0
RWKV_HEADS = 8
RWKV_HEAD = 64
RWKV_WIDTH = RWKV_HEADS * RWKV_HEAD
RWKV_PAIRS = RWKV_HEADS // 2
DECAY_LORA = 64
AAA_LORA = 64
GATE_LORA = 128
RWKV_PROJ = 3 * RWKV_WIDTH + DECAY_LORA + AAA_LORA + GATE_LORA
LNX_EPS = 64e-5
D_FF = 4 * D_MODEL
NORM_EPS = 1e-6

LANES = 128
SUBLANES = 8
HEAD_SLAB = LANES
ROPE_LANE0 = QK_NOPE
HALF_ROPE = QK_ROPE // 2
VMEM_LIMIT_BYTES = 56 * 1024 * 1024

ROW_TILE = 256
FFN_ROW_TILE = 512
FFN_CHUNK = 1024
ATTN_TILE = 512
SCAN_CHUNK = 64
DECODE_SLOTS = 2
DECODE_DMA_UNROLL = 8
DECODE_KEY_SPLITS = 4
WKV_STEP_SEQS = 8


def _params(*sem):
    return pltpu.CompilerParams(dimension_semantics=sem, vmem_limit_bytes=VMEM_LIMIT_BYTES)


def _bdot(a, b):
    return jnp.dot(a.astype(BF16), b.astype(BF16), preferred_element_type=F32)


def _bdot_nt(a, b):
    return lax.dot_general(a.astype(BF16), b.astype(BF16), (((1,), (1,)), ((), ())),
                           preferred_element_type=F32)


def _bdot_tn(a, b):
    return lax.dot_general(a.astype(BF16), b.astype(BF16), (((0,), (0,)), ((), ())),
                           preferred_element_type=F32)


def _rms(x, g):
    return x * lax.rsqrt(jnp.mean(x * x, axis=-1, keepdims=True) + NORM_EPS) * g


def _split_dot(x, ones_rhs, parts):
    acc = None
    rem = x
    for _ in range(parts):
        piece = rem.astype(BF16)
        rem = rem - piece.astype(F32)
        d = jnp.dot(piece, ones_rhs, preferred_element_type=F32)
        acc = d if acc is None else acc + d
    return acc


def _ones_split_dot(ones_lhs, x, parts):
    acc = None
    rem = x
    for _ in range(parts):
        piece = rem.astype(BF16)
        rem = rem - piece.astype(F32)
        d = jnp.dot(ones_lhs, piece, preferred_element_type=F32)
        acc = d if acc is None else acc + d
    return acc


def _const_spec(shape):
    nd = len(shape)
    return pl.BlockSpec(shape, lambda *_: (0,) * nd)


def _mod_spec(mod, tm, tiles_per_group):
    _, r, d = mod.shape
    if r == 1:
        return pl.BlockSpec((None, 1, d), lambda i: (i // tiles_per_group, 0, 0))
    return pl.BlockSpec((None, tm, d), lambda i: (i // tiles_per_group, i % tiles_per_group, 0))


def _ada_kernel(c_ref, w_ref, b_ref, o_ref):
    c = c_ref[...]
    o_ref[...] = _bdot(c * jax.nn.sigmoid(c), w_ref[...]) + b_ref[...]


def _adaln(c_all, w_ada, b_ada):
    depth, d, n = w_ada.shape
    rows = c_all.shape[0]
    tn = n // 4
    return pl.pallas_call(
        _ada_kernel,
        grid=(depth, n // tn),
        in_specs=[pl.BlockSpec((rows, d), lambda l, j: (0, 0)),
                  pl.BlockSpec((None, d, tn), lambda l, j: (l, 0, j)),
                  pl.BlockSpec((None, 1, tn), lambda l, j: (l, 0, j))],
        out_specs=pl.BlockSpec((None, rows, tn), lambda l, j: (l, 0, j)),
        out_shape=jax.ShapeDtypeStruct((depth, rows, n), F32),
        compiler_params=_params("arbitrary", "arbitrary"),
        name="adaln",
    )(c_all, w_ada, b_ada.reshape(depth, 1, n))


def _rope_slab(x, cos, sin_lo, sin_hi):
    return (x * cos + pltpu.roll(x, LANES - HALF_ROPE, 1) * sin_lo
            + pltpu.roll(x, HALF_ROPE, 1) * sin_hi)


def _inproj_kernel(x_ref, sh_ref, sc_ref, gpre_ref, cos_ref, slo_ref, shi_ref,
                   wq_ref, wckv_ref, wkpe_ref, wrw_ref, wgate_ref, gq_ref, wuq_ref,
                   gkv_ref, wukn_ref, wuvt_ref,
                   q_ref, k_ref, vt_ref, ckv_ref, kpe_ref, zrw_ref, gate_ref):
    x = x_ref[...]
    h = _rms(x, gpre_ref[...]) * (1.0 + sc_ref[...]) + sh_ref[...]
    hb = h.astype(BF16)
    zrw_ref[...] = jnp.dot(hb, wrw_ref[...], preferred_element_type=F32)
    gate_ref[...] = jax.nn.sigmoid(jnp.dot(hb, wgate_ref[...], preferred_element_type=F32))
    cos, slo, shi = cos_ref[...], slo_ref[...], shi_ref[...]

    q_lat = jnp.dot(hb, wq_ref[...], preferred_element_type=F32)
    q = _bdot(_rms(q_lat, gq_ref[...]), wuq_ref[...])
    for hd in range(MLA_HEADS):
        sl = slice(hd * HEAD_SLAB, (hd + 1) * HEAD_SLAB)
        q_ref[:, sl] = _rope_slab(q[:, sl], cos, slo, shi).astype(BF16)

    ckv = _rms(jnp.dot(hb, wckv_ref[...], preferred_element_type=F32), gkv_ref[...])
    ckv_ref[...] = ckv
    cb = ckv.astype(BF16)
    kpe = _rope_slab(jnp.dot(hb, wkpe_ref[...], preferred_element_type=F32), cos, slo, shi)
    kpe_ref[...] = kpe[:, ROPE_LANE0:ROPE_LANE0 + QK_ROPE]
    kn = jnp.dot(cb, wukn_ref[...], preferred_element_type=F32)
    for hd in range(MLA_HEADS):
        sl = slice(hd * HEAD_SLAB, (hd + 1) * HEAD_SLAB)
        k_ref[:, sl] = (kn[:, sl] + kpe).astype(BF16)
    vt_ref[...] = _bdot_nt(wuvt_ref[...], cb).astype(BF16)


def _inproj(x, sh, sc, g_pre, rope_tabs, lw, rows_per_group):
    t, d = x.shape
    tm = min(ROW_TILE, rows_per_group)
    tpg = rows_per_group // tm
    cos, slo, shi = rope_tabs
    tab_spec = pl.BlockSpec((tm, LANES), lambda i: (i % tpg, 0))
    row = lambda n: pl.BlockSpec((tm, n), lambda i: (i, 0))
    weights = [lw["w_q"], lw["w_ckv"], lw["w_kpe"], lw["w_rw"], lw["w_gate"], lw["g_q"],
               lw["w_uq"], lw["g_kv"], lw["w_ukn"], lw["w_uv_t"]]
    slab = MLA_HEADS * HEAD_SLAB
    vt_spec = pl.BlockSpec((None, MLA_WIDTH, tm), lambda i: (i // tpg, 0, i % tpg))
    return pl.pallas_call(
        _inproj_kernel,
        grid=(t // tm,),
        in_specs=[row(d), _mod_spec(sh, tm, tpg), _mod_spec(sc, tm, tpg), _const_spec(g_pre.shape),
                  tab_spec, tab_spec, tab_spec] + [_const_spec(w.shape) for w in weights],
        out_specs=[row(slab), row(slab), vt_spec, row(KV_RANK), row(QK_ROPE),
                   row(RWKV_PROJ), row(2 * d)],
        out_shape=[jax.ShapeDtypeStruct((t, slab), BF16), jax.ShapeDtypeStruct((t, slab), BF16),
                   jax.ShapeDtypeStruct((t // rows_per_group, MLA_WIDTH, rows_per_group), BF16),
                   jax.ShapeDtypeStruct((t, KV_RANK), F32),
                   jax.ShapeDtypeStruct((t, QK_ROPE), F32), jax.ShapeDtypeStruct((t, RWKV_PROJ), F32),
                   jax.ShapeDtypeStruct((t, 2 * d), F32)],
        compiler_params=_params("arbitrary"),
        name="inproj",
    )(x, sh, sc, g_pre, cos, slo, shi, *weights)


def _attn_block(q_ref, k_ref, vt_ref, m_sc, l_sc, acc_sc, valid):
    def scores(hd):
        sl = slice(hd * HEAD_SLAB, (hd + 1) * HEAD_SLAB)
        return _bdot_nt(k_ref[:, sl], q_ref[:, sl])

    ahead = scores(0)
    for hd in range(MLA_HEADS):
        rows = slice(hd * V_HEAD, (hd + 1) * V_HEAD)
        s = ahead * (ATTN_SCALE * LOG2_E)
        if hd + 1 < MLA_HEADS:
            ahead = scores(hd + 1)
        if valid is not None:
            s = jnp.where(valid, s, -jnp.inf)
        m_prev = m_sc[hd]
        m_new = jnp.maximum(m_prev, jnp.max(s, axis=0, keepdims=True))
        alpha = jnp.exp2(m_prev - m_new)
        p = jnp.exp2(s - m_new)
        l_sc[hd] = alpha * l_sc[hd] + jnp.sum(p, axis=0, keepdims=True)
        m_sc[hd] = m_new
        acc_sc[rows, :] = alpha * acc_sc[rows, :] + jnp.dot(vt_ref[rows, :], p.astype(BF16),
                                                           preferred_element_type=F32)


def _attn_kernel(q_ref, k_ref, vt_ref, o_ref, m_sc, l_sc, acc_sc, *, tile):
    i = pl.program_id(1)
    j = pl.program_id(2)

    @pl.when(j == 0)
    def _():
        m_sc[...] = jnp.full(m_sc.shape, -jnp.inf, F32)
        l_sc[...] = jnp.zeros(l_sc.shape, F32)
        acc_sc[...] = jnp.zeros(acc_sc.shape, F32)

    @pl.when(j < i)
    def _():
        _attn_block(q_ref, k_ref, vt_ref, m_sc, l_sc, acc_sc, None)

    @pl.when(j == i)
    def _():
        key = lax.broadcasted_iota(jnp.int32, (tile, tile), 0)
        qry = lax.broadcasted_iota(jnp.int32, (tile, tile), 1)
        _attn_block(q_ref, k_ref, vt_ref, m_sc, l_sc, acc_sc, key <= qry)
        for hd in range(MLA_HEADS):
            rows = slice(hd * V_HEAD, (hd + 1) * V_HEAD)
            acc_sc[rows, :] = acc_sc[rows, :] * (1.0 / l_sc[hd])
        o_ref[...] = acc_sc[...].T.astype(o_ref.dtype)


def _prompt_attention(q, k, vt, batch, seq):
    tile = min(ATTN_TILE, seq)
    nt = seq // tile
    slab = MLA_HEADS * HEAD_SLAB
    q3, k3 = (a.reshape(batch, seq, slab) for a in (q, k))
    out = pl.pallas_call(
        functools.partial(_attn_kernel, tile=tile),
        grid=(batch, nt, nt),
        in_specs=[pl.BlockSpec((None, tile, slab), lambda b, i, j: (b, i, 0)),
                  pl.BlockSpec((None, tile, slab), lambda b, i, j: (b, jnp.minimum(i, j), 0)),
                  pl.BlockSpec((None, MLA_WIDTH, tile), lambda b, i, j: (b, 0, jnp.minimum(i, j)))],
        out_specs=pl.BlockSpec((None, tile, MLA_WIDTH), lambda b, i, j: (b, i, 0)),
        out_shape=jax.ShapeDtypeStruct((batch, seq, MLA_WIDTH), BF16),
        scratch_shapes=[pltpu.VMEM((MLA_HEADS, 1, tile), F32), pltpu.VMEM((MLA_HEADS, 1, tile), F32),
                        pltpu.VMEM((MLA_WIDTH, tile), F32)],
        compiler_params=_params("arbitrary", "arbitrary", "arbitrary"),
        name="prompt_attention",
    )(q3, k3, vt)
    return out.reshape(batch * seq, MLA_WIDTH)


def _qlat_kernel(q_ref, wuk_ref, o_ref):
    for hd in range(MLA_HEADS):
        o_ref[hd] = jnp.dot(q_ref[:, hd * HEAD_SLAB:(hd + 1) * HEAD_SLAB], wuk_ref[hd],
                            preferred_element_type=F32)


def _decode_kernel(pt_ref, qlat_ref, qlt_ref, qpe_ref, cnew_ref, knew_ref, ckv_hbm, kpet_hbm, o_ref,
                   ckv_buf, kpet_buf, sems, *, layer, n_pages, page):
    b = pl.program_id(0)
    nb = pl.num_programs(0)

    def page_copies(seq, slot, pg):
        pid = pt_ref[seq * n_pages + pg]
        rows = pl.ds(pl.multiple_of(pg * page, page), page)
        return (pltpu.make_async_copy(ckv_hbm.at[layer, pid], ckv_buf.at[slot, rows, :], sems.at[0, slot]),
                pltpu.make_async_copy(kpet_hbm.at[layer, pid], kpet_buf.at[slot, :, rows], sems.at[1, slot]))

    def start_gather(seq, slot):
        def body(pg, carry):
            for cp in page_copies(seq, slot, pg):
                cp.start()
            return carry
        lax.fori_loop(0, n_pages, body, 0, unroll=DECODE_DMA_UNROLL)

    def wait_gather(seq, slot):
        def body(pg, carry):
            for cp in page_copies(seq, slot, pg):
                cp.wait()
            return carry
        lax.fori_loop(0, n_pages, body, 0, unroll=DECODE_DMA_UNROLL)

    slot = b % DECODE_SLOTS

    @pl.when(b == 0)
    def _():
        start_gather(b, slot)

    @pl.when(b + 1 < nb)
    def _():
        start_gather(b + 1, (b + 1) % DECODE_SLOTS)

    wait_gather(b, slot)

    ql, qp = qlat_ref[...], qpe_ref[...]
    cn, kn = cnew_ref[...], knew_ref[...]
    qpb = qp.astype(BF16)
    qlt = qlt_ref[...]
    m_run = (jnp.sum(ql * cn, axis=-1, keepdims=True)
             + jnp.sum(qp * kn, axis=-1, keepdims=True)) * ATTN_SCALE
    l_run = jnp.ones_like(m_run)
    acc = jnp.broadcast_to(cn, (MLA_HEADS, KV_RANK))
    span = n_pages * page // DECODE_KEY_SPLITS

    def scores(i):
        sp = pl.ds(i * span, span)
        ck = ckv_buf[slot, sp, :].astype(BF16)
        s_t = jnp.dot(ck, qlt, preferred_element_type=F32)
        return ck, (s_t.T[:MLA_HEADS] + _bdot(qpb, kpet_buf[slot, :, sp])) * ATTN_SCALE

    ahead = scores(0)
    for i in range(DECODE_KEY_SPLITS):
        ck, s = ahead
        if i + 1 < DECODE_KEY_SPLITS:
            ahead = scores(i + 1)
        m_new = jnp.maximum(m_run, jnp.max(s, axis=-1, keepdims=True))
        alpha = jnp.exp(m_run - m_new)
        p = jnp.exp(s - m_new)
        l_run = alpha * l_run + jnp.sum(p, axis=-1, keepdims=True)
        acc = alpha * acc + jnp.dot(p.astype(BF16), ck, preferred_element_type=F32)
        m_run = m_new
    o_ref[...] = acc / l_run


def _olat_kernel(o_ref, wuv_ref, y_ref):
    acc = None
    for hd in range(MLA_HEADS):
        d = _bdot(o_ref[hd], wuv_ref[hd])
        acc = d if acc is None else acc + d
    y_ref[...] = acc.astype(y_ref.dtype)


def _sample_attention(layer, q_slab, ckv_new, kpe_new, cache_ckv, cache_kpe_t, page_table, lw):
    b = q_slab.shape[0]
    n_pages = page_table.shape[1]
    page = cache_ckv.shape[2]
    past_len = n_pages * page
    qlat = pl.pallas_call(
        _qlat_kernel,
        out_shape=jax.ShapeDtypeStruct((MLA_HEADS, b, KV_RANK), F32),
        compiler_params=pltpu.CompilerParams(vmem_limit_bytes=VMEM_LIMIT_BYTES),
        name="decode_qlat",
    )(q_slab, lw["w_uk_t"])
    qlat_t = jnp.pad(qlat.transpose(1, 2, 0), ((0, 0), (0, 0), (0, LANES - MLA_HEADS))).astype(BF16)
    qlat = qlat.transpose(1, 0, 2)
    qpe = q_slab.reshape(b, MLA_HEADS, HEAD_SLAB)[:, :, ROPE_LANE0:ROPE_LANE0 + QK_ROPE].astype(F32)

    per_seq = lambda shape: pl.BlockSpec((None,) + shape, lambda bi, pt: (bi, 0, 0))
    grid_spec = pltpu.PrefetchScalarGridSpec(
        num_scalar_prefetch=1,
        grid=(b,),
        in_specs=[per_seq((MLA_HEADS, KV_RANK)), per_seq((KV_RANK, LANES)), per_seq((MLA_HEADS, QK_ROPE)),
                  per_seq((1, KV_RANK)), per_seq((1, QK_ROPE)),
                  pl.BlockSpec(memory_space=pl.ANY), pl.BlockSpec(memory_space=pl.ANY)],
        out_specs=per_seq((MLA_HEADS, KV_RANK)),
        scratch_shapes=[pltpu.VMEM((DECODE_SLOTS, past_len, KV_RANK), F32),
                        pltpu.VMEM((DECODE_SLOTS, QK_ROPE, past_len), F32),
                        pltpu.SemaphoreType.DMA((2, DECODE_SLOTS))],
    )
    olat = pl.pallas_call(
        functools.partial(_decode_kernel, layer=layer, n_pages=n_pages, page=page),
        grid_spec=grid_spec,
        out_shape=jax.ShapeDtypeStruct((b, MLA_HEADS, KV_RANK), F32),
        compiler_params=_params("arbitrary"),
        name="decode_attention",
    )(page_table.reshape(-1), qlat, qlat_t, qpe, ckv_new.reshape(b, 1, KV_RANK), kpe_new.reshape(b, 1, QK_ROPE),
      cache_ckv, cache_kpe_t)
    return pl.pallas_call(
        _olat_kernel,
        out_shape=jax.ShapeDtypeStruct((b, MLA_WIDTH), BF16),
        compiler_params=pltpu.CompilerParams(vmem_limit_bytes=VMEM_LIMIT_BYTES),
        name="decode_out",
    )(olat.transpose(1, 0, 2), lw["w_uv_wide"])


def _softplus(x):
    return jnp.maximum(x, 0.0) + jnp.log(1.0 + jnp.exp(-jnp.abs(x)))


def _rwkv_token_ops(zm, w0, w2, a0, a2, g2, k_k, k_a, r_k, seg):
    w = RWKV_WIDTH
    r, k, v = zm[:, :w], zm[:, w:2 * w], zm[:, 2 * w:3 * w]
    xwa = zm[:, 3 * w:3 * w + DECAY_LORA + AAA_LORA]
    xg = zm[:, 3 * w + DECAY_LORA + AAA_LORA:]
    w_log = -_softplus(-(w0 + _bdot(jnp.tanh(xwa), w2))) - 0.5
    a = jax.nn.sigmoid(a0 + _bdot(xwa, a2))
    g = _bdot(jax.nn.sigmoid(xg), g2)
    kk = k * k_k
    kk = kk / jnp.maximum(jnp.sqrt(_split_dot(kk * kk, seg, 2)), 1e-12)
    k2 = k * (1.0 + (a - 1.0) * k_a)
    bonus = _split_dot(r * k2 * r_k, seg, 2) * v
    return r, -jnp.exp(w_log), k2, v, -kk, kk * a, g, bonus


def _rwkv_consts(lw):
    return [lw["mu"], lw["rw_w0"], lw["rw_w2"], lw["rw_a0"], lw["rw_a2"], lw["rw_g2"], lw["rw_k_k"],
            lw["rw_k_a"], lw["rw_r_k"], lw["seg_ones"]]


def _token_prep_kernel(z_ref, zprev_ref, mu_ref, w0_ref, w2_ref, a0_ref, a2_ref, g2_ref, kk_ref, ka_ref,
                       rk_ref, seg_ref, *out_refs):
    z = z_ref[...]
    zm = z + mu_ref[...] * (zprev_ref[...] - z)
    outs = _rwkv_token_ops(zm, w0_ref[...], w2_ref[...], a0_ref[...], a2_ref[...], g2_ref[...],
                           kk_ref[...], ka_ref[...], rk_ref[...], seg_ref[...])
    for o_ref, val in zip(out_refs, outs):
        o_ref[...] = val


def _token_prep(z_rw, z_prev, lw):
    t = z_rw.shape[0]
    tm = min(ROW_TILE, t)
    row = lambda n: pl.BlockSpec((tm, n), lambda i: (i, 0))
    consts = _rwkv_consts(lw)
    return pl.pallas_call(
        _token_prep_kernel,
        grid=(t // tm,),
        in_specs=[row(RWKV_PROJ), row(RWKV_PROJ)] + [_const_spec(c.shape) for c in consts],
        out_specs=[row(RWKV_WIDTH)] * 8,
        out_shape=[jax.ShapeDtypeStruct((t, RWKV_WIDTH), F32)] * 8,
        compiler_params=_params("arbitrary"),
        name="rwkv_token_prep",
    )(z_rw, z_prev, *consts)


def _pair_norm_gate(y, seg_pair, lnw, lnb, bonus, g):
    inv_n = 1.0 / RWKV_HEAD
    cen = y - _split_dot(y, seg_pair, 2) * inv_n
    var = _split_dot(cen * cen, seg_pair, 2) * inv_n
    return (cen * lax.rsqrt(var + LNX_EPS) * lnw + lnb + bonus) * g


def _chunk_prep_kernel(z_ref, s0_ref, mu_ref, w0_ref, w2_ref, a0_ref, a2_ref, g2_ref, kk_ref, ka_ref,
                       rk_ref, seg_ref,
                       w_ref, r_ref, bh_ref, kh_ref, v_ref, mrb_ref, u0_ref, y0_ref, g_ref, bonus_ref,
                       elast_ref, last_ref, carry_sc, *, chunk):
    i = pl.program_id(1)
    z = z_ref[...]
    rows = z.shape[0]
    first = jnp.where(i == 0, s0_ref[...], carry_sc[...])
    is_row0 = lax.broadcasted_iota(jnp.int32, z.shape, 0) == 0
    z_prev = jnp.where(is_row0, first, pltpu.roll(z, 1, 0))
    carry_sc[...] = z[rows - 1:rows, :]
    last_ref[...] = z[rows - 1:rows, :]
    zm = z + mu_ref[...] * (z_prev - z)
    r, lw, k2, v, an, bn, g, bonus = _rwkv_token_ops(
        zm, w0_ref[...], w2_ref[...], a0_ref[...], a2_ref[...], g2_ref[...], kk_ref[...], ka_ref[...],
        rk_ref[...], seg_ref[...])
    g_ref[...] = g
    bonus_ref[...] = bonus
    v_ref[...] = v.astype(BF16)

    ri = lax.broadcasted_iota(jnp.int32, (rows, rows), 0)
    ci = lax.broadcasted_iota(jnp.int32, (rows, rows), 1)
    chunk_start = ri - jnp.bitwise_and(ri, chunk - 1)
    tri = jnp.where((ci <= ri) & (ci >= chunk_start), 1.0, 0.0).astype(BF16)
    cum = _ones_split_dot(tri, lw, 3)
    n_chunks = rows // chunk
    lasts = [cum[(c + 1) * chunk - 1:(c + 1) * chunk, :] for c in range(n_chunks)]
    cum_last = jnp.concatenate([jnp.broadcast_to(x, (chunk, RWKV_WIDTH)) for x in lasts], axis=0)
    for c in range(n_chunks):
        elast_ref[c] = jnp.broadcast_to(jnp.exp(lasts[c]), (SUBLANES, RWKV_WIDTH))
    e_neg = jnp.exp(-cum)
    e_tail = jnp.exp(cum_last - cum)
    a_t = an * jnp.exp(cum - lw)
    r_t = r * jnp.exp(cum)
    b_t, k_t = bn * e_neg, k2 * e_neg
    r_ref[...] = r_t.astype(BF16)
    bh_ref[...] = (bn * e_tail).astype(BF16)
    kh_ref[...] = (k2 * e_tail).astype(BF16)

    lane = lax.broadcasted_iota(jnp.int32, (chunk, LANES), 1)
    tok = lax.broadcasted_iota(jnp.int32, (chunk, LANES), 0)
    left = lane < RWKV_HEAD
    col = jnp.bitwise_and(lane, RWKV_HEAD - 1)
    strict, incl = col < tok, col <= tok
    zero = jnp.zeros((chunk, LANES), F32)
    insts = [(slice(c * chunk, (c + 1) * chunk), slice(p * LANES, (p + 1) * LANES))
             for c in range(n_chunks) for p in range(RWKV_PAIRS)]

    prods = []
    for rs, ls in insts:
        ar = jnp.concatenate([a_t[rs, ls], r_t[rs, ls]], axis=0)
        bp, kp = b_t[rs, ls], k_t[rs, ls]
        bk = jnp.concatenate([jnp.where(left, bp, zero), jnp.where(left, zero, bp),
                              jnp.where(left, kp, zero), jnp.where(left, zero, kp)], axis=0)
        prods.append(_bdot_nt(ar, bk))
    l_pows, sols = [], []
    for (rs, ls), pq in zip(insts, prods):
        ab, ak = pq[:chunk, :LANES], pq[:chunk, LANES:]
        rb, rk = pq[chunk:, :LANES], pq[chunk:, LANES:]
        mrb_ref[rs, ls] = jnp.where(incl, rb, 0.0).astype(BF16)
        vp = v[rs, ls]
        v_bd = jnp.concatenate([jnp.where(left, vp, zero), jnp.where(left, zero, vp)], axis=0)
        lm = jnp.concatenate([jnp.where(strict, ak, 0.0), jnp.where(incl, rk, 0.0)], axis=0)
        lmv = _bdot(lm, v_bd)
        y0_ref[rs, ls] = lmv[chunk:]
        lv = lmv[:chunk]
        ap = a_t[rs, ls]
        sols.append(jnp.concatenate([jnp.where(left, ap, pltpu.roll(lv, RWKV_HEAD, 1)),
                                     jnp.where(left, pltpu.roll(ap, RWKV_HEAD, 1), lv)], axis=0))
        l_pows.append(jnp.concatenate([jnp.where(strict & left, ab, 0.0),
                                       jnp.where(strict & jnp.logical_not(left), ab, 0.0)], axis=0))
    span = 1
    while span < chunk:
        sols = [x + _bdot(lp, x) for lp, x in zip(l_pows, sols)]
        span *= 2
        if span < chunk:
            l_pows = [_bdot(lp, lp) for lp in l_pows]
    for (rs, ls), x in zip(insts, sols):
        top, bot = x[:chunk], x[chunk:]
        w_ref[rs, ls] = jnp.where(left, top, pltpu.roll(bot, RWKV_HEAD, 1)).astype(BF16)
        u0_ref[rs, ls] = jnp.where(left, pltpu.roll(top, RWKV_HEAD, 1), bot)


def _chunk_prep(z_rw, shift0, lw, batch, seq):
    chunk = min(SCAN_CHUNK, seq)
    tm = min(ROW_TILE, seq)
    tiles = seq // tm
    cpt = tm // chunk
    n_chunks = seq // chunk
    row = lambda n: pl.BlockSpec((tm, n), lambda b, i: (b * tiles + i, 0))
    per_seq = pl.BlockSpec((None, 1, RWKV_PROJ), lambda b, i: (b, 0, 0))
    consts = _rwkv_consts(lw)
    t = batch * seq
    wide = lambda dt: jax.ShapeDtypeStruct((t, RWKV_WIDTH), dt)
    outs = pl.pallas_call(
        functools.partial(_chunk_prep_kernel, chunk=chunk),
        grid=(batch, tiles),
        in_specs=[row(RWKV_PROJ), per_seq] + [_const_spec(c.shape) for c in consts],
        out_specs=[row(RWKV_WIDTH)] * 10
                  + [pl.BlockSpec((None, cpt, SUBLANES, RWKV_WIDTH), lambda b, i: (b, i, 0, 0)), per_seq],
        out_shape=[wide(BF16)] * 6 + [wide(F32)] * 4
                  + [jax.ShapeDtypeStruct((batch, n_chunks, SUBLANES, RWKV_WIDTH), F32),
                     jax.ShapeDtypeStruct((batch, 1, RWKV_PROJ), F32)],
        scratch_shapes=[pltpu.VMEM((1, RWKV_PROJ), F32)],
        compiler_params=_params("arbitrary", "arbitrary"),
        name="rwkv_chunk_prep",
    )(z_rw, shift0.reshape(batch, 1, RWKV_PROJ), *consts)
    return outs[:10], outs[10], outs[11]


def _chunk_scan_kernel(w_ref, r_ref, bh_ref, kh_ref, v_ref, mrb_ref, u0_ref, y0_ref, g_ref, bonus_ref,
                       elast_ref, s0_ref, lnw_ref, lnb_ref, seg_ref, y_ref, sout_ref, state_sc, *, chunk):
    c = pl.program_id(0)
    nb = w_ref.shape[0]
    hh = RWKV_HEAD
    insts = [(b, p, slice(p * LANES, (p + 1) * LANES)) for b in range(nb) for p in range(RWKV_PAIRS)]

    @pl.when(c == 0)
    def _():
        state_sc[...] = jnp.zeros(state_sc.shape, F32)
        for b, p, _ in insts:
            state_sc[b, p, :hh, :hh] = s0_ref[b, 2 * p]
            state_sc[b, p, hh:, hh:] = s0_ref[b, 2 * p + 1]

    lane = lax.broadcasted_iota(jnp.int32, (chunk, LANES), 1)
    left = lane < hh
    zero = jnp.zeros((chunk, LANES), F32)
    vi = lax.broadcasted_iota(jnp.int32, (LANES, LANES), 0)
    ki = lax.broadcasted_iota(jnp.int32, (LANES, LANES), 1)
    same_head = (vi < hh) == (ki < hh)
    seg = seg_ref[...]

    ps = [_bdot_nt(jnp.concatenate([w_ref[b, :, ls], r_ref[b, :, ls]], axis=0), state_sc[b, p])
          for b, p, ls in insts]
    ys = []
    for (b, p, ls), wr_s in zip(insts, ps):
        u = wr_s[:chunk] + u0_ref[b, :, ls]
        u_bd = jnp.concatenate([jnp.where(left, u, zero), jnp.where(left, zero, u)], axis=0)
        ys.append(wr_s[chunk:] + _bdot(mrb_ref[b, :, ls], u_bd) + y0_ref[b, :, ls])
        uv = jnp.concatenate([u.astype(BF16), v_ref[b, :, ls]], axis=0)
        bk = jnp.concatenate([bh_ref[b, :, ls], kh_ref[b, :, ls]], axis=0)
        upd = _bdot_tn(uv, bk)
        state_sc[b, p] = state_sc[b, p] * elast_ref[b, 0:1, ls] + jnp.where(same_head, upd, 0.0)
    for (b, p, ls), y in zip(insts, ys):
        y_ref[b, :, ls] = _pair_norm_gate(y, seg, lnw_ref[:, ls], lnb_ref[:, ls], bonus_ref[b, :, ls],
                                          g_ref[b, :, ls]).astype(y_ref.dtype)

    @pl.when(c == pl.num_programs(0) - 1)
    def _():
        for b, p, _ in insts:
            sout_ref[b, 2 * p] = state_sc[b, p, :hh, :hh]
            sout_ref[b, 2 * p + 1] = state_sc[b, p, hh:, hh:]


def _chunk_scan(chunk_ops, elast, state0, lw, batch, seq):
    chunk = min(SCAN_CHUNK, seq)
    n_chunks = seq // chunk
    tok = pl.BlockSpec((batch, chunk, RWKV_WIDTH), lambda c: (0, c, 0))
    ops3 = [a.reshape(batch, seq, RWKV_WIDTH) for a in chunk_ops]
    y, s_new = pl.pallas_call(
        functools.partial(_chunk_scan_kernel, chunk=chunk),
        grid=(n_chunks,),
        in_specs=[tok] * 10 + [pl.BlockSpec((batch, SUBLANES, RWKV_WIDTH), lambda c: (0, c, 0)),
                               _const_spec(state0.shape), _const_spec(lw["lnx_w"].shape),
                               _const_spec(lw["lnx_b"].shape), _const_spec(lw["seg_pair"].shape)],
        out_specs=[tok, _const_spec(state0.shape)],
        out_shape=[jax.ShapeDtypeStruct((batch, seq, RWKV_WIDTH), BF16),
                   jax.ShapeDtypeStruct(state0.shape, F32)],
        scratch_shapes=[pltpu.VMEM((batch, RWKV_PAIRS, LANES, LANES), F32)],
        compiler_params=_params("arbitrary"),
        name="rwkv_chunk_scan",
    )(*ops3, elast.reshape(batch, n_chunks * SUBLANES, RWKV_WIDTH), state0, lw["lnx_w"], lw["lnx_b"],
      lw["seg_pair"])
    return y.reshape(batch * seq, RWKV_WIDTH), s_new


def _wkv_step_kernel(r_ref, lw_ref, k_ref, v_ref, an_ref, bn_ref, g_ref, bonus_ref, s_ref, lnw_ref, lnb_ref,
                     seg_ref, y_ref, sout_ref, yraw_sc):
    n = RWKV_HEAD
    eye = (lax.broadcasted_iota(jnp.int32, (n, n), 0) == lax.broadcasted_iota(jnp.int32, (n, n), 1))
    ones = seg_ref[:n, :n]
    insts = [(sq, hd, slice(sq, sq + 1), slice(hd * n, (hd + 1) * n))
             for sq in range(s_ref.shape[0]) for hd in range(RWKV_HEADS)]
    count = len(insts)
    lhs = ([s_ref[sq, hd] * an_ref[rw, sl] for sq, hd, rw, sl in insts]
           + [jnp.where(eye, v_ref[rw, sl], 0.0) for _, _, rw, sl in insts])
    sums = _split_dot(jnp.concatenate(lhs, axis=0), ones, 3)
    weighted = []
    for idx, (sq, hd, rw, sl) in enumerate(insts):
        sa = sums[idx * n:(idx + 1) * n]
        v_rows = sums[(count + idx) * n:(count + idx + 1) * n]
        new = s_ref[sq, hd] * jnp.exp(lw_ref[rw, sl]) + sa * bn_ref[rw, sl] + v_rows * k_ref[rw, sl]
        sout_ref[sq, hd] = new
        weighted.append(new * r_ref[rw, sl])
    y_sums = _split_dot(jnp.concatenate(weighted, axis=0), ones, 2)
    for idx, (sq, hd, rw, sl) in enumerate(insts):
        yraw_sc[rw, sl] = jnp.sum(jnp.where(eye, y_sums[idx * n:(idx + 1) * n], 0.0), axis=0, keepdims=True)
    y = yraw_sc[...]
    seg = seg_ref[...]
    inv_n = 1.0 / n
    cen = y - _split_dot(y, seg, 2) * inv_n
    var = _split_dot(cen * cen, seg, 2) * inv_n
    y_ref[...] = (cen * lax.rsqrt(var + LNX_EPS) * lnw_ref[...] + lnb_ref[...] + bonus_ref[...]) * g_ref[...]


def _wkv_step(layer, prep, state_all, lw):
    b = state_all.shape[1]
    per = min(WKV_STEP_SEQS, b)
    tok = pl.BlockSpec((per, RWKV_WIDTH), lambda i: (i, 0))
    st_shape = (per, RWKV_HEADS, RWKV_HEAD, RWKV_HEAD)
    return pl.pallas_call(
        _wkv_step_kernel,
        grid=(b // per,),
        in_specs=[tok] * 8 + [pl.BlockSpec((None,) + st_shape, lambda i: (layer, i, 0, 0, 0)),
                              _const_spec(lw["lnx_w"].shape), _const_spec(lw["lnx_b"].shape),
                              _const_spec(lw["seg_ones"].shape)],
        out_specs=[tok, pl.BlockSpec(st_shape, lambda i: (i, 0, 0, 0))],
        out_shape=[jax.ShapeDtypeStruct((b, RWKV_WIDTH), F32), jax.ShapeDtypeStruct(state_all.shape[1:], F32)],
        scratch_shapes=[pltpu.VMEM((per, RWKV_WIDTH), F32)],
        compiler_params=_params("arbitrary"),
        name="wkv_step",
    )(*prep, state_all, lw["lnx_w"], lw["lnx_b"], lw["seg_ones"])


def _merge_kernel(x_ref, ym_ref, yr_ref, gate_ref, gt_ref, gpost_ref, wbm_ref, wbr_ref, wout_ref, o_ref):
    d = x_ref.shape[-1]
    merged = (gate_ref[:, :d] * _bdot(ym_ref[...], wbm_ref[...])
              + gate_ref[:, d:] * _bdot(yr_ref[...], wbr_ref[...]))
    out = _bdot(merged, wout_ref[...])
    o_ref[...] = x_ref[...] + gt_ref[...] * _rms(out, gpost_ref[...])


def _merge(x, y_mla, y_rw, gate, gt, g_post, lw, rows_per_group):
    t, d = x.shape
    tm = min(ROW_TILE, rows_per_group)
    tpg = rows_per_group // tm
    row = lambda n: pl.BlockSpec((tm, n), lambda i: (i, 0))
    weights = [lw["w_branch_mla"], lw["w_branch_rwkv"], lw["w_out"]]
    return pl.pallas_call(
        _merge_kernel,
        grid=(t // tm,),
        in_specs=[row(d), row(MLA_WIDTH), row(RWKV_WIDTH), row(2 * d), _mod_spec(gt, tm, tpg),
                  _const_spec(g_post.shape)] + [_const_spec(w.shape) for w in weights],
        out_specs=row(d),
        out_shape=jax.ShapeDtypeStruct((t, d), F32),
        compiler_params=_params("arbitrary"),
        name="merge",
    )(x, y_mla, y_rw, gate, gt, g_post, *weights)


def _ffn_kernel(x_ref, sh_ref, sc_ref, gt_ref, gpre_ref, gpost_ref, w1_ref, w2_ref, o_ref):
    x = x_ref[...]
    hb = (_rms(x, gpre_ref[...]) * (1.0 + sc_ref[...]) + sh_ref[...]).astype(BF16)
    acc = jnp.zeros(x.shape, F32)
    for c0 in range(0, w1_ref.shape[1], FFN_CHUNK):
        a = jnp.maximum(jnp.dot(hb, w1_ref[:, c0:c0 + FFN_CHUNK], preferred_element_type=F32), 0.0)
        acc = acc + _bdot(a * a, w2_ref[c0:c0 + FFN_CHUNK, :])
    o_ref[...] = x + gt_ref[...] * _rms(acc, gpost_ref[...])


def _ffn(layer, x, sh, sc, gt, g_pre, g_post, w_ff1_all, w_ff2_all, rows_per_group):
    t, d = x.shape
    tm = min(FFN_ROW_TILE, rows_per_group)
    tpg = rows_per_group // tm
    row = pl.BlockSpec((tm, d), lambda i: (i, 0))
    of_layer = lambda w: pl.BlockSpec((None,) + w.shape[1:], lambda i: (layer, 0, 0))
    return pl.pallas_call(
        _ffn_kernel,
        grid=(t // tm,),
        in_specs=[row, _mod_spec(sh, tm, tpg), _mod_spec(sc, tm, tpg), _mod_spec(gt, tm, tpg),
                  _const_spec(g_pre.shape), _const_spec(g_post.shape),
                  of_layer(w_ff1_all), of_layer(w_ff2_all)],
        out_specs=row,
        out_shape=jax.ShapeDtypeStruct((t, d), F32),
        compiler_params=_params("arbitrary"),
        name="ffn",
    )(x, sh, sc, gt, g_pre, g_post, w_ff1_all, w_ff2_all)


def _rope_tables(pos):
    inv = ROPE_BASE ** (-jnp.arange(HALF_ROPE, dtype=F32) / HALF_ROPE)
    ang = pos.astype(F32)[:, None] * inv[None, :]
    cos, sin = jnp.cos(ang), jnp.sin(ang)
    n = pos.shape[0]
    ones = jnp.ones((n, ROPE_LANE0), F32)
    zeros = jnp.zeros((n, ROPE_LANE0), F32)
    pad = jnp.zeros((n, LANES - ROPE_LANE0 - QK_ROPE), F32)
    z16 = jnp.zeros((n, HALF_ROPE), F32)
    cos_t = jnp.concatenate([ones, cos, cos, pad], axis=1)
    sin_lo = jnp.concatenate([zeros, -sin, z16, pad], axis=1)
    sin_hi = jnp.concatenate([zeros, z16, sin, pad], axis=1)
    return cos_t, sin_lo, sin_hi


def _layer_weights(l, p):
    w_in = p["w_in"][l]
    w_q = w_in[:, :Q_RANK]
    w_ckv = w_in[:, Q_RANK:Q_RANK + KV_RANK]
    w_kpe = jnp.pad(w_in[:, Q_RANK + KV_RANK:MLA_PROJ], ((0, 0), (ROPE_LANE0, LANES - ROPE_LANE0 - QK_ROPE)))
    w_rw = w_in[:, MLA_PROJ:MLA_PROJ + RWKV_PROJ]
    w_gate = w_in[:, MLA_PROJ + RWKV_PROJ:]
    qk = QK_NOPE + QK_ROPE
    w_uq = jnp.pad(p["w_uq"][l].reshape(Q_RANK, MLA_HEADS, qk), ((0, 0), (0, 0), (0, HEAD_SLAB - qk)))
    w_ukv = p["w_ukv"][l].reshape(KV_RANK, MLA_HEADS, QK_NOPE + V_HEAD)
    w_uk, w_uv = w_ukv[..., :QK_NOPE], w_ukv[..., QK_NOPE:]
    w_ukn = jnp.pad(w_uk, ((0, 0), (0, 0), (0, HEAD_SLAB - QK_NOPE)))
    w_uk_t = jnp.pad(w_uk.transpose(1, 2, 0), ((0, 0), (0, HEAD_SLAB - QK_NOPE), (0, 0)))
    eye = jnp.eye(MLA_HEADS, dtype=F32)
    w_uv_wide = (w_uv.transpose(1, 0, 2)[:, :, None, :] * eye[:, None, :, None]).reshape(
        MLA_HEADS, KV_RANK, MLA_WIDTH)
    lora_pad_lo = ((0, AAA_LORA), (0, 0))
    lora_pad_hi = ((DECAY_LORA, 0), (0, 0))
    heads = jnp.arange(RWKV_WIDTH) // RWKV_HEAD
    seg_ones = (heads[:, None] == heads[None, :]).astype(BF16)
    row2 = lambda a: a.reshape(1, -1)
    return {
        "w_q": w_q.astype(BF16), "w_ckv": w_ckv.astype(BF16), "w_kpe": w_kpe.astype(BF16),
        "w_rw": w_rw.astype(BF16), "w_gate": w_gate.astype(BF16),
        "g_q": row2(p["g_q_norm"][l]), "g_kv": row2(p["g_kv_norm"][l]),
        "w_uq": w_uq.reshape(Q_RANK, MLA_HEADS * HEAD_SLAB).astype(BF16),
        "w_ukn": w_ukn.reshape(KV_RANK, MLA_HEADS * HEAD_SLAB).astype(BF16),
        "w_uv_t": w_uv.reshape(KV_RANK, MLA_WIDTH).T.astype(BF16),
        "w_uk_t": w_uk_t.astype(BF16), "w_uv_wide": w_uv_wide.astype(BF16),
        "mu": row2(p["mu_shift"][l]),
        "rw_w0": row2(p["rw_w0"][l]), "rw_a0": row2(p["rw_a0"][l]),
        "rw_w2": jnp.pad(p["rw_w2"][l], lora_pad_lo).astype(BF16),
        "rw_a2": jnp.pad(p["rw_a2"][l], lora_pad_hi).astype(BF16),
        "rw_g2": p["rw_g2"][l].astype(BF16),
        "rw_k_k": row2(p["rw_k_k"][l]), "rw_k_a": row2(p["rw_k_a"][l]), "rw_r_k": row2(p["rw_r_k"][l]),
        "seg_ones": seg_ones, "seg_pair": seg_ones[:LANES, :LANES],
        "lnx_w": row2(p["rw_lnx_w"][l]), "lnx_b": row2(p["rw_lnx_b"][l]),
        "w_branch_mla": p["w_branch_mla"][l].astype(BF16), "w_branch_rwkv": p["w_branch_rwkv"][l].astype(BF16),
        "w_out": p["w_out"][l].astype(BF16),
        "g_pre_mix": row2(p["g_pre_mix"][l]), "g_post_mix": row2(p["g_post_mix"][l]),
        "g_pre_ff": row2(p["g_pre_ff"][l]), "g_post_ff": row2(p["g_post_ff"][l]),
    }


def _trunk(x, mods, rope_tabs, rows_per_group, layer_weights, w_ff1_all, w_ff2_all, mixers):
    ckv_rows, kpe_rows, wkv_states, shift_rows = [], [], [], []
    for l in range(DEPTH):
        lw = layer_weights[l]
        sh1, sc1, gt1, sh2, sc2, gt2 = mods[l]
        q, k, vt, ckv, kpe, z_rw, gate = _inproj(x, sh1, sc1, lw["g_pre_mix"], rope_tabs, lw, rows_per_group)
        y_mla, y_rw, wkv_new, shift_new = mixers(l, lw, q, k, vt, ckv, kpe, z_rw)
        x = _merge(x, y_mla, y_rw, gate, gt1, lw["g_post_mix"], lw, rows_per_group)
        x = _ffn(l, x, sh2, sc2, gt2, lw["g_pre_ff"], lw["g_post_ff"], w_ff1_all, w_ff2_all, rows_per_group)
        ckv_rows.append(ckv)
        kpe_rows.append(kpe)
        wkv_states.append(wkv_new)
        shift_rows.append(shift_new)
    return x, jnp.stack(ckv_rows), jnp.stack(kpe_rows), jnp.stack(wkv_states), jnp.stack(shift_rows)


def kernel(x_prompt, x_sample, c_prompt, c_sample, cache_ckv, cache_kpe, state_wkv, state_shift, page_table, w_ada, b_ada, g_pre_mix, g_post_mix, g_pre_ff, g_post_ff, w_in, mu_shift, g_q_norm, w_uq, g_kv_norm, w_ukv, rw_w0, rw_w2, rw_a0, rw_a2, rw_g2, rw_k_k, rw_k_a, rw_r_k, rw_lnx_w, rw_lnx_b, w_branch_mla, w_branch_rwkv, w_out, w_ff1, w_ff2):
    params = dict(g_pre_mix=g_pre_mix, g_post_mix=g_post_mix, g_pre_ff=g_pre_ff, g_post_ff=g_post_ff,
                  w_in=w_in, mu_shift=mu_shift, g_q_norm=g_q_norm, w_uq=w_uq, g_kv_norm=g_kv_norm,
                  w_ukv=w_ukv, rw_w0=rw_w0, rw_w2=rw_w2, rw_a0=rw_a0, rw_a2=rw_a2, rw_g2=rw_g2,
                  rw_k_k=rw_k_k, rw_k_a=rw_k_a, rw_r_k=rw_r_k, rw_lnx_w=rw_lnx_w, rw_lnx_b=rw_lnx_b,
                  w_branch_mla=w_branch_mla, w_branch_rwkv=w_branch_rwkv, w_out=w_out,
                  w_ff1=w_ff1, w_ff2=w_ff2)
    b_p, s_p, d = x_prompt.shape
    b_s, s_s, _ = x_sample.shape
    assert s_s == 1, "the sample group advances one token per sequence"
    past_len = page_table.shape[1] * cache_ckv.shape[2]
    layer_weights = [_layer_weights(l, params) for l in range(DEPTH)]
    w_ff1_all, w_ff2_all = w_ff1.astype(BF16), w_ff2.astype(BF16)

    mod = _adaln(jnp.concatenate([c_prompt, c_sample], axis=0), w_ada, b_ada)
    mod = mod.reshape(DEPTH, b_p + b_s, 6, d)
    mods_p = [[mod[l, :b_p, j].reshape(b_p, 1, d) for j in range(6)] for l in range(DEPTH)]
    mods_s = [[mod[l, b_p:, j].reshape(1, b_s, d) for j in range(6)] for l in range(DEPTH)]

    shift0_p = jnp.zeros((b_p, RWKV_PROJ), F32)
    wkv0_p = jnp.zeros((b_p, RWKV_HEADS, RWKV_HEAD, RWKV_HEAD), F32)

    def prompt_mixers(l, lw, q, k, vt, ckv, kpe, z_rw):
        y_mla = _prompt_attention(q, k, vt, b_p, s_p)
        chunk_ops, elast, last = _chunk_prep(z_rw, shift0_p, lw, b_p, s_p)
        y_rw, wkv_new = _chunk_scan(chunk_ops, elast, wkv0_p, lw, b_p, s_p)
        return y_mla, y_rw, wkv_new, last.reshape(b_p, RWKV_PROJ)

    y_p, p_ckv, p_kpe, p_wkv, p_shift = _trunk(
        x_prompt.reshape(b_p * s_p, d), mods_p, _rope_tables(jnp.arange(s_p)), s_p, layer_weights,
        w_ff1_all, w_ff2_all, prompt_mixers)

    cache_kpe_t = cache_kpe.transpose(0, 1, 3, 2)

    def sample_mixers(l, lw, q, k, vt, ckv, kpe, z_rw):
        y_mla = _sample_attention(l, q, ckv, kpe, cache_ckv, cache_kpe_t, page_table, lw)
        prep = _token_prep(z_rw, state_shift[l], lw)
        y_rw, wkv_new = _wkv_step(l, prep, state_wkv, lw)
        return y_mla, y_rw, wkv_new, z_rw

    y_s, s_ckv, s_kpe, s_wkv, s_shift = _trunk(
        x_sample.reshape(b_s, d), mods_s, _rope_tables(jnp.full((b_s,), past_len)), b_s, layer_weights,
        w_ff1_all, w_ff2_all, sample_mixers)

    return (y_p.reshape(b_p, s_p, d), y_s.reshape(b_s, 1, d),
            p_ckv.reshape(DEPTH, b_p, s_p, KV_RANK), p_kpe.reshape(DEPTH, b_p, s_p, QK_ROPE), p_wkv, p_shift,
            s_ckv.reshape(DEPTH, b_s, 1, KV_RANK), s_kpe.reshape(DEPTH, b_s, 1, QK_ROPE), s_wkv, s_shift)
```

```python
import functools

import jax
import jax.numpy as jnp
from jax import lax
from jax.experimental import pallas as pl
from jax.experimental.pallas import tpu as pltpu

F32 = jnp.float32
BF16 = jnp.bfloat16

D_MODEL = 1024
DEPTH = 4
MLA_HEADS = 8
QK_NOPE = 64
QK_ROPE = 32
V_HEAD = 64
Q_RANK = 384
KV_RANK = 256
MLA_WIDTH = MLA_HEADS * V_HEAD
MLA_PROJ = Q_RANK + KV_RANK + QK_ROPE
ATTN_SCALE = (QK_NOPE + QK_ROPE) ** -0.5
LOG2_E = 1.4426950408889634
ROPE_BASE = 10000.0
RWKV_HEADS = 8
RWKV_HEAD = 64
RWKV_WIDTH = RWKV_HEADS * RWKV_HEAD
RWKV_PAIRS = RWKV_HEADS // 2
DECAY_LORA = 64
AAA_LORA = 64
GATE_LORA = 128
RWKV_PROJ = 3 * RWKV_WIDTH + DECAY_LORA + AAA_LORA + GATE_LORA
LNX_EPS = 64e-5
D_FF = 4 * D_MODEL
NORM_EPS = 1e-6

LANES = 128
SUBLANES = 8
HEAD_SLAB = LANES
ROPE_LANE0 = QK_NOPE
HALF_ROPE = QK_ROPE // 2
VMEM_LIMIT_BYTES = 56 * 1024 * 1024

ROW_TILE = 256
INPROJ_ROW_TILE = 256
MERGE_ROW_TILE = 512
FFN_ROW_TILE = 512
FFN_CHUNK = 1024
ATTN_TILE = 512
SCAN_CHUNK = 64
DECODE_SLOTS = 2
DECODE_DMA_UNROLL = 8
DECODE_KEY_SPLITS = 4
WKV_VALUE_UNROLL = 8


def _params(*sem):
    return pltpu.CompilerParams(dimension_semantics=sem, vmem_limit_bytes=VMEM_LIMIT_BYTES)


def _bdot(a, b):
    return jnp.dot(a.astype(BF16), b.astype(BF16), preferred_element_type=F32)


def _bdot_nt(a, b):
    return lax.dot_general(a.astype(BF16), b.astype(BF16), (((1,), (1,)), ((), ())),
                           preferred_element_type=F32)


def _bdot_tn(a, b):
    return lax.dot_general(a.astype(BF16), b.astype(BF16), (((0,), (0,)), ((), ())),
                           preferred_element_type=F32)


def _rms(x, g):
    return x * lax.rsqrt(jnp.mean(x * x, axis=-1, keepdims=True) + NORM_EPS) * g


def _split_dot(x, ones_rhs, parts):
    acc = None
    rem = x
    for _ in range(parts):
        piece = rem.astype(BF16)
        rem = rem - piece.astype(F32)
        d = jnp.dot(piece, ones_rhs, preferred_element_type=F32)
        acc = d if acc is None else acc + d
    return acc


def _ones_split_dot(ones_lhs, x, parts):
    acc = None
    rem = x
    for _ in range(parts):
        piece = rem.astype(BF16)
        rem = rem - piece.astype(F32)
        d = jnp.dot(ones_lhs, piece, preferred_element_type=F32)
        acc = d if acc is None else acc + d
    return acc


def _const_spec(shape):
    nd = len(shape)
    return pl.BlockSpec(shape, lambda *_: (0,) * nd)


def _mod_spec(mod, tm, tiles_per_group):
    _, r, d = mod.shape
    if r == 1:
        return pl.BlockSpec((None, 1, d), lambda i: (i // tiles_per_group, 0, 0))
    return pl.BlockSpec((None, tm, d), lambda i: (i // tiles_per_group, i % tiles_per_group, 0))


def _ada_kernel(c_ref, w_ref, b_ref, o_ref):
    c = c_ref[...]
    o_ref[...] = _bdot(c * jax.nn.sigmoid(c), w_ref[...]) + b_ref[...]


def _adaln(c_all, w_ada, b_ada):
    depth, d, n = w_ada.shape
    rows = c_all.shape[0]
    tn = n // 4
    return pl.pallas_call(
        _ada_kernel,
        grid=(depth, n // tn),
        in_specs=[pl.BlockSpec((rows, d), lambda l, j: (0, 0)),
                  pl.BlockSpec((None, d, tn), lambda l, j: (l, 0, j)),
                  pl.BlockSpec((None, 1, tn), lambda l, j: (l, 0, j))],
        out_specs=pl.BlockSpec((None, rows, tn), lambda l, j: (l, 0, j)),
        out_shape=jax.ShapeDtypeStruct((depth, rows, n), F32),
        compiler_params=_params("arbitrary", "arbitrary"),
        name="adaln",
    )(c_all, w_ada, b_ada.reshape(depth, 1, n))


def _rope_slab(x, cos, sin_lo, sin_hi):
    return (x * cos + pltpu.roll(x, LANES - HALF_ROPE, 1) * sin_lo
            + pltpu.roll(x, HALF_ROPE, 1) * sin_hi)


def _inproj_kernel(x_ref, sh_ref, sc_ref, gpre_ref, cos_ref, slo_ref, shi_ref,
                   wq_ref, wckv_ref, wkpe_ref, wrw_ref, wgate_ref, gq_ref, wuq_ref,
                   gkv_ref, wukn_ref, wuvt_ref,
                   q_ref, k_ref, vt_ref, ckv_ref, kpe_ref, zrw_ref, gate_ref):
    x = x_ref[...]
    h = _rms(x, gpre_ref[...]) * (1.0 + sc_ref[...]) + sh_ref[...]
    hb = h.astype(BF16)
    zrw_ref[...] = jnp.dot(hb, wrw_ref[...], preferred_element_type=F32)
    gate_ref[...] = jax.nn.sigmoid(jnp.dot(hb, wgate_ref[...], preferred_element_type=F32))
    cos, slo, shi = cos_ref[...], slo_ref[...], shi_ref[...]

    q_lat = jnp.dot(hb, wq_ref[...], preferred_element_type=F32)
    q = _bdot(_rms(q_lat, gq_ref[...]), wuq_ref[...])
    for hd in range(MLA_HEADS):
        sl = slice(hd * HEAD_SLAB, (hd + 1) * HEAD_SLAB)
        q_ref[:, sl] = _rope_slab(q[:, sl], cos, slo, shi).astype(BF16)

    ckv = _rms(jnp.dot(hb, wckv_ref[...], preferred_element_type=F32), gkv_ref[...])
    ckv_ref[...] = ckv
    cb = ckv.astype(BF16)
    kpe = _rope_slab(jnp.dot(hb, wkpe_ref[...], preferred_element_type=F32), cos, slo, shi)
    kpe_ref[...] = kpe[:, ROPE_LANE0:ROPE_LANE0 + QK_ROPE]
    kn = jnp.dot(cb, wukn_ref[...], preferred_element_type=F32)
    for hd in range(MLA_HEADS):
        sl = slice(hd * HEAD_SLAB, (hd + 1) * HEAD_SLAB)
        k_ref[:, sl] = (kn[:, sl] + kpe).astype(BF16)
    vt_ref[...] = _bdot_nt(wuvt_ref[...], cb).astype(BF16)


def _inproj(x, sh, sc, g_pre, rope_tabs, lw, rows_per_group):
    t, d = x.shape
    tm = min(INPROJ_ROW_TILE, rows_per_group)
    tpg = rows_per_group // tm
    cos, slo, shi = rope_tabs
    tab_spec = pl.BlockSpec((tm, LANES), lambda i: (i % tpg, 0))
    row = lambda n: pl.BlockSpec((tm, n), lambda i: (i, 0))
    weights = [lw["w_q"], lw["w_ckv"], lw["w_kpe"], lw["w_rw"], lw["w_gate"], lw["g_q"],
               lw["w_uq"], lw["g_kv"], lw["w_ukn"], lw["w_uv_t"]]
    slab = MLA_HEADS * HEAD_SLAB
    vt_spec = pl.BlockSpec((None, MLA_WIDTH, tm), lambda i: (i // tpg, 0, i % tpg))
    return pl.pallas_call(
        _inproj_kernel,
        grid=(t // tm,),
        in_specs=[row(d), _mod_spec(sh, tm, tpg), _mod_spec(sc, tm, tpg), _const_spec(g_pre.shape),
                  tab_spec, tab_spec, tab_spec] + [_const_spec(w.shape) for w in weights],
        out_specs=[row(slab), row(slab), vt_spec, row(KV_RANK), row(QK_ROPE),
                   row(RWKV_PROJ), row(2 * d)],
        out_shape=[jax.ShapeDtypeStruct((t, slab), BF16), jax.ShapeDtypeStruct((t, slab), BF16),
                   jax.ShapeDtypeStruct((t // rows_per_group, MLA_WIDTH, rows_per_group), BF16),
                   jax.ShapeDtypeStruct((t, KV_RANK), F32),
                   jax.ShapeDtypeStruct((t, QK_ROPE), F32), jax.ShapeDtypeStruct((t, RWKV_PROJ), F32),
                   jax.ShapeDtypeStruct((t, 2 * d), F32)],
        compiler_params=_params("arbitrary"),
        name="inproj",
    )(x, sh, sc, g_pre, cos, slo, shi, *weights)


def _attn_block(q_ref, k_ref, vt_ref, m_sc, l_sc, acc_sc, valid):
    def scores(hd):
        sl = slice(hd * HEAD_SLAB, (hd + 1) * HEAD_SLAB)
        return _bdot_nt(k_ref[:, sl], q_ref[:, sl])

    ahead = scores(0)
    for hd in range(MLA_HEADS):
        rows = slice(hd * V_HEAD, (hd + 1) * V_HEAD)
        s = ahead * (ATTN_SCALE * LOG2_E)
        if hd + 1 < MLA_HEADS:
            ahead = scores(hd + 1)
        if valid is not None:
            s = jnp.where(valid, s, -jnp.inf)
        m_prev = m_sc[hd]
        m_new = jnp.maximum(m_prev, jnp.max(s, axis=0, keepdims=True))
        alpha = jnp.exp2(m_prev - m_new)
        p = jnp.exp2(s - m_new)
        l_sc[hd] = alpha * l_sc[hd] + jnp.sum(p, axis=0, keepdims=True)
        m_sc[hd] = m_new
        acc_sc[rows, :] = alpha * acc_sc[rows, :] + jnp.dot(vt_ref[rows, :], p.astype(BF16),
                                                           preferred_element_type=F32)


def _attn_kernel(q_ref, k_ref, vt_ref, o_ref, m_sc, l_sc, acc_sc, *, tile):
    i = pl.program_id(1)
    j = pl.program_id(2)

    @pl.when(j == 0)
    def _():
        m_sc[...] = jnp.full(m_sc.shape, -jnp.inf, F32)
        l_sc[...] = jnp.zeros(l_sc.shape, F32)
        acc_sc[...] = jnp.zeros(acc_sc.shape, F32)

    @pl.when(j < i)
    def _():
        _attn_block(q_ref, k_ref, vt_ref, m_sc, l_sc, acc_sc, None)

    @pl.when(j == i)
    def _():
        key = lax.broadcasted_iota(jnp.int32, (tile, tile), 0)
        qry = lax.broadcasted_iota(jnp.int32, (tile, tile), 1)
        _attn_block(q_ref, k_ref, vt_ref, m_sc, l_sc, acc_sc, key <= qry)
        for hd in range(MLA_HEADS):
            rows = slice(hd * V_HEAD, (hd + 1) * V_HEAD)
            acc_sc[rows, :] = acc_sc[rows, :] * (1.0 / l_sc[hd])
        o_ref[...] = acc_sc[...].T.astype(o_ref.dtype)


def _prompt_attention(q, k, vt, batch, seq):
    tile = min(ATTN_TILE, seq)
    nt = seq // tile
    slab = MLA_HEADS * HEAD_SLAB
    q3, k3 = (a.reshape(batch, seq, slab) for a in (q, k))
    out = pl.pallas_call(
        functools.partial(_attn_kernel, tile=tile),
        grid=(batch, nt, nt),
        in_specs=[pl.BlockSpec((None, tile, slab), lambda b, i, j: (b, i, 0)),
                  pl.BlockSpec((None, tile, slab), lambda b, i, j: (b, jnp.minimum(i, j), 0)),
                  pl.BlockSpec((None, MLA_WIDTH, tile), lambda b, i, j: (b, 0, jnp.minimum(i, j)))],
        out_specs=pl.BlockSpec((None, tile, MLA_WIDTH), lambda b, i, j: (b, i, 0)),
        out_shape=jax.ShapeDtypeStruct((batch, seq, MLA_WIDTH), BF16),
        scratch_shapes=[pltpu.VMEM((MLA_HEADS, 1, tile), F32), pltpu.VMEM((MLA_HEADS, 1, tile), F32),
                        pltpu.VMEM((MLA_WIDTH, tile), F32)],
        compiler_params=_params("arbitrary", "arbitrary", "arbitrary"),
        name="prompt_attention",
    )(q3, k3, vt)
    return out.reshape(batch * seq, MLA_WIDTH)


def _qlat_kernel(q_ref, wuk_ref, o_ref):
    for hd in range(MLA_HEADS):
        o_ref[hd] = jnp.dot(q_ref[:, hd * HEAD_SLAB:(hd + 1) * HEAD_SLAB], wuk_ref[hd],
                            preferred_element_type=F32)


def _decode_kernel(pt_ref, qlat_ref, qlt_ref, qpe_ref, cnew_ref, knew_ref, ckv_hbm, kpet_hbm, o_ref,
                   ckv_buf, kpet_buf, sems, *, layer, n_pages, page):
    b = pl.program_id(0)
    nb = pl.num_programs(0)

    def page_copies(seq, slot, pg):
        pid = pt_ref[seq * n_pages + pg]
        rows = pl.ds(pl.multiple_of(pg * page, page), page)
        return (pltpu.make_async_copy(ckv_hbm.at[layer, pid], ckv_buf.at[slot, rows, :], sems.at[0, slot]),
                pltpu.make_async_copy(kpet_hbm.at[layer, pid], kpet_buf.at[slot, :, rows], sems.at[1, slot]))

    def start_gather(seq, slot):
        def body(pg, carry):
            for cp in page_copies(seq, slot, pg):
                cp.start()
            return carry
        lax.fori_loop(0, n_pages, body, 0, unroll=DECODE_DMA_UNROLL)

    def wait_gather(seq, slot):
        def body(pg, carry):
            for cp in page_copies(seq, slot, pg):
                cp.wait()
            return carry
        lax.fori_loop(0, n_pages, body, 0, unroll=DECODE_DMA_UNROLL)

    slot = b % DECODE_SLOTS

    @pl.when(b == 0)
    def _():
        start_gather(b, slot)

    @pl.when(b + 1 < nb)
    def _():
        start_gather(b + 1, (b + 1) % DECODE_SLOTS)

    wait_gather(b, slot)

    ql, qp = qlat_ref[...], qpe_ref[...]
    cn, kn = cnew_ref[...], knew_ref[...]
    qpb = qp.astype(BF16)
    qlt = qlt_ref[...]
    m_run = (jnp.sum(ql * cn, axis=-1, keepdims=True)
             + jnp.sum(qp * kn, axis=-1, keepdims=True)) * ATTN_SCALE
    l_run = jnp.ones_like(m_run)
    acc = jnp.broadcast_to(cn, (MLA_HEADS, KV_RANK))
    span = n_pages * page // DECODE_KEY_SPLITS

    def scores(i):
        sp = pl.ds(i * span, span)
        ck = ckv_buf[slot, sp, :].astype(BF16)
        s_t = jnp.dot(ck, qlt, preferred_element_type=F32)
        return ck, (s_t.T[:MLA_HEADS] + _bdot(qpb, kpet_buf[slot, :, sp])) * ATTN_SCALE

    ahead = scores(0)
    for i in range(DECODE_KEY_SPLITS):
        ck, s = ahead
        if i + 1 < DECODE_KEY_SPLITS:
            ahead = scores(i + 1)
        m_new = jnp.maximum(m_run, jnp.max(s, axis=-1, keepdims=True))
        alpha = jnp.exp(m_run - m_new)
        p = jnp.exp(s - m_new)
        l_run = alpha * l_run + jnp.sum(p, axis=-1, keepdims=True)
        acc = alpha * acc + jnp.dot(p.astype(BF16), ck, preferred_element_type=F32)
        m_run = m_new
    o_ref[...] = acc / l_run


def _olat_kernel(o_ref, wuv_ref, y_ref):
    acc = None
    for hd in range(MLA_HEADS):
        d = _bdot(o_ref[hd], wuv_ref[hd])
        acc = d if acc is None else acc + d
    y_ref[...] = acc.astype(y_ref.dtype)


def _sample_attention(layer, q_slab, ckv_new, kpe_new, cache_ckv, cache_kpe_t, page_table, lw):
    b = q_slab.shape[0]
    n_pages = page_table.shape[1]
    page = cache_ckv.shape[2]
    past_len = n_pages * page
    qlat = pl.pallas_call(
        _qlat_kernel,
        out_shape=jax.ShapeDtypeStruct((MLA_HEADS, b, KV_RANK), F32),
        compiler_params=pltpu.CompilerParams(vmem_limit_bytes=VMEM_LIMIT_BYTES),
        name="decode_qlat",
    )(q_slab, lw["w_uk_t"])
    qlat_t = jnp.pad(qlat.transpose(1, 2, 0), ((0, 0), (0, 0), (0, LANES - MLA_HEADS))).astype(BF16)
    qlat = qlat.transpose(1, 0, 2)
    qpe = q_slab.reshape(b, MLA_HEADS, HEAD_SLAB)[:, :, ROPE_LANE0:ROPE_LANE0 + QK_ROPE].astype(F32)

    per_seq = lambda shape: pl.BlockSpec((None,) + shape, lambda bi, pt: (bi, 0, 0))
    grid_spec = pltpu.PrefetchScalarGridSpec(
        num_scalar_prefetch=1,
        grid=(b,),
        in_specs=[per_seq((MLA_HEADS, KV_RANK)), per_seq((KV_RANK, LANES)), per_seq((MLA_HEADS, QK_ROPE)),
                  per_seq((1, KV_RANK)), per_seq((1, QK_ROPE)),
                  pl.BlockSpec(memory_space=pl.ANY), pl.BlockSpec(memory_space=pl.ANY)],
        out_specs=per_seq((MLA_HEADS, KV_RANK)),
        scratch_shapes=[pltpu.VMEM((DECODE_SLOTS, past_len, KV_RANK), F32),
                        pltpu.VMEM((DECODE_SLOTS, QK_ROPE, past_len), F32),
                        pltpu.SemaphoreType.DMA((2, DECODE_SLOTS))],
    )
    olat = pl.pallas_call(
        functools.partial(_decode_kernel, layer=layer, n_pages=n_pages, page=page),
        grid_spec=grid_spec,
        out_shape=jax.ShapeDtypeStruct((b, MLA_HEADS, KV_RANK), F32),
        compiler_params=_params("arbitrary"),
        name="decode_attention",
    )(page_table.reshape(-1), qlat, qlat_t, qpe, ckv_new.reshape(b, 1, KV_RANK), kpe_new.reshape(b, 1, QK_ROPE),
      cache_ckv, cache_kpe_t)
    return pl.pallas_call(
        _olat_kernel,
        out_shape=jax.ShapeDtypeStruct((b, MLA_WIDTH), BF16),
        compiler_params=pltpu.CompilerParams(vmem_limit_bytes=VMEM_LIMIT_BYTES),
        name="decode_out",
    )(olat.transpose(1, 0, 2), lw["w_uv_wide"])


def _softplus(x):
    return jnp.maximum(x, 0.0) + jnp.log(1.0 + jnp.exp(-jnp.abs(x)))


def _rwkv_token_ops(zm, w0, w2, a0, a2, g2, k_k, k_a, r_k, seg):
    w = RWKV_WIDTH
    r, k, v = zm[:, :w], zm[:, w:2 * w], zm[:, 2 * w:3 * w]
    xwa = zm[:, 3 * w:3 * w + DECAY_LORA + AAA_LORA]
    xg = zm[:, 3 * w + DECAY_LORA + AAA_LORA:]
    w_log = -_softplus(-(w0 + _bdot(jnp.tanh(xwa), w2))) - 0.5
    a = jax.nn.sigmoid(a0 + _bdot(xwa, a2))
    g = _bdot(jax.nn.sigmoid(xg), g2)
    kk = k * k_k
    kk = kk / jnp.maximum(jnp.sqrt(_split_dot(kk * kk, seg, 2)), 1e-12)
    k2 = k * (1.0 + (a - 1.0) * k_a)
    bonus = _split_dot(r * k2 * r_k, seg, 2) * v
    return r, -jnp.exp(w_log), k2, v, -kk, kk * a, g, bonus


def _rwkv_consts(lw):
    return [lw["mu"], lw["rw_w0"], lw["rw_w2"], lw["rw_a0"], lw["rw_a2"], lw["rw_g2"], lw["rw_k_k"],
            lw["rw_k_a"], lw["rw_r_k"], lw["seg_ones"]]


def _token_prep_kernel(z_ref, zprev_ref, mu_ref, w0_ref, w2_ref, a0_ref, a2_ref, g2_ref, kk_ref, ka_ref,
                       rk_ref, seg_ref, *out_refs):
    z = z_ref[...]
    zm = z + mu_ref[...] * (zprev_ref[...] - z)
    outs = _rwkv_token_ops(zm, w0_ref[...], w2_ref[...], a0_ref[...], a2_ref[...], g2_ref[...],
                           kk_ref[...], ka_ref[...], rk_ref[...], seg_ref[...])
    for o_ref, val in zip(out_refs, outs):
        o_ref[...] = val.T


def _token_prep(z_rw, z_prev, lw):
    t = z_rw.shape[0]
    tm = min(ROW_TILE, t)
    row = lambda n: pl.BlockSpec((tm, n), lambda i: (i, 0))
    consts = _rwkv_consts(lw)
    return pl.pallas_call(
        _token_prep_kernel,
        grid=(t // tm,),
        in_specs=[row(RWKV_PROJ), row(RWKV_PROJ)] + [_const_spec(c.shape) for c in consts],
        out_specs=[pl.BlockSpec((RWKV_WIDTH, tm), lambda i: (0, i))] * 8,
        out_shape=[jax.ShapeDtypeStruct((RWKV_WIDTH, t), F32)] * 8,
        compiler_params=_params("arbitrary"),
        name="rwkv_token_prep",
    )(z_rw, z_prev, *consts)


def _pair_norm_gate(y, seg_pair, lnw, lnb, bonus, g):
    inv_n = 1.0 / RWKV_HEAD
    cen = y - _split_dot(y, seg_pair, 2) * inv_n
    var = _split_dot(cen * cen, seg_pair, 2) * inv_n
    return (cen * lax.rsqrt(var + LNX_EPS) * lnw + lnb + bonus) * g


def _chunk_prep_kernel(z_ref, s0_ref, mu_ref, w0_ref, w2_ref, a0_ref, a2_ref, g2_ref, kk_ref, ka_ref,
                       rk_ref, seg_ref,
                       w_ref, r_ref, bh_ref, kh_ref, v_ref, mrb_ref, u0_ref, y0_ref, g_ref, bonus_ref,
                       elast_ref, last_ref, carry_sc, *, chunk):
    i = pl.program_id(1)
    z = z_ref[...]
    rows = z.shape[0]
    first = jnp.where(i == 0, s0_ref[...], carry_sc[...])
    is_row0 = lax.broadcasted_iota(jnp.int32, z.shape, 0) == 0
    z_prev = jnp.where(is_row0, first, pltpu.roll(z, 1, 0))
    carry_sc[...] = z[rows - 1:rows, :]
    last_ref[...] = z[rows - 1:rows, :]
    zm = z + mu_ref[...] * (z_prev - z)
    r, lw, k2, v, an, bn, g, bonus = _rwkv_token_ops(
        zm, w0_ref[...], w2_ref[...], a0_ref[...], a2_ref[...], g2_ref[...], kk_ref[...], ka_ref[...],
        rk_ref[...], seg_ref[...])
    g_ref[...] = g
    bonus_ref[...] = bonus
    v_ref[...] = v.astype(BF16)

    ri = lax.broadcasted_iota(jnp.int32, (rows, rows), 0)
    ci = lax.broadcasted_iota(jnp.int32, (rows, rows), 1)
    chunk_start = ri - jnp.bitwise_and(ri, chunk - 1)
    tri = jnp.where((ci <= ri) & (ci >= chunk_start), 1.0, 0.0).astype(BF16)
    cum = _ones_split_dot(tri, lw, 3)
    n_chunks = rows // chunk
    lasts = [cum[(c + 1) * chunk - 1:(c + 1) * chunk, :] for c in range(n_chunks)]
    cum_last = jnp.concatenate([jnp.broadcast_to(x, (chunk, RWKV_WIDTH)) for x in lasts], axis=0)
    for c in range(n_chunks):
        elast_ref[c] = jnp.broadcast_to(jnp.exp(lasts[c]), (SUBLANES, RWKV_WIDTH))
    e_neg = jnp.exp(-cum)
    e_tail = jnp.exp(cum_last - cum)
    a_t = an * jnp.exp(cum - lw)
    r_t = r * jnp.exp(cum)
    b_t, k_t = bn * e_neg, k2 * e_neg
    r_ref[...] = r_t.astype(BF16)
    bh_ref[...] = (bn * e_tail).astype(BF16)
    kh_ref[...] = (k2 * e_tail).astype(BF16)

    lane = lax.broadcasted_iota(jnp.int32, (chunk, LANES), 1)
    tok = lax.broadcasted_iota(jnp.int32, (chunk, LANES), 0)
    left = lane < RWKV_HEAD
    col = jnp.bitwise_and(lane, RWKV_HEAD - 1)
    strict, incl = col < tok, col <= tok
    zero = jnp.zeros((chunk, LANES), F32)
    insts = [(slice(c * chunk, (c + 1) * chunk), slice(p * LANES, (p + 1) * LANES))
             for c in range(n_chunks) for p in range(RWKV_PAIRS)]

    prods = []
    for rs, ls in insts:
        ar = jnp.concatenate([a_t[rs, ls], r_t[rs, ls]], axis=0)
        bp, kp = b_t[rs, ls], k_t[rs, ls]
        bk = jnp.concatenate([jnp.where(left, bp, zero), jnp.where(left, zero, bp),
                              jnp.where(left, kp, zero), jnp.where(left, zero, kp)], axis=0)
        prods.append(_bdot_nt(ar, bk))
    l_pows, sols = [], []
    for (rs, ls), pq in zip(insts, prods):
        ab, ak = pq[:chunk, :LANES], pq[:chunk, LANES:]
        rb, rk = pq[chunk:, :LANES], pq[chunk:, LANES:]
        mrb_ref[rs, ls] = jnp.where(incl, rb, 0.0).astype(BF16)
        vp = v[rs, ls]
        v_bd = jnp.concatenate([jnp.where(left, vp, zero), jnp.where(left, zero, vp)], axis=0)
        lm = jnp.concatenate([jnp.where(strict, ak, 0.0), jnp.where(incl, rk, 0.0)], axis=0)
        lmv = _bdot(lm, v_bd)
        y0_ref[rs, ls] = lmv[chunk:]
        lv = lmv[:chunk]
        ap = a_t[rs, ls]
        sols.append(jnp.concatenate([jnp.where(left, ap, pltpu.roll(lv, RWKV_HEAD, 1)),
                                     jnp.where(left, pltpu.roll(ap, RWKV_HEAD, 1), lv)], axis=0))
        l_pows.append(jnp.concatenate([jnp.where(strict & left, ab, 0.0),
                                       jnp.where(strict & jnp.logical_not(left), ab, 0.0)], axis=0))
    span = 1
    while span < chunk:
        sols = [x + _bdot(lp, x) for lp, x in zip(l_pows, sols)]
        span *= 2
        if span < chunk:
            l_pows = [_bdot(lp, lp) for lp in l_pows]
    for (rs, ls), x in zip(insts, sols):
        top, bot = x[:chunk], x[chunk:]
        w_ref[rs, ls] = jnp.where(left, top, pltpu.roll(bot, RWKV_HEAD, 1)).astype(BF16)
        u0_ref[rs, ls] = jnp.where(left, pltpu.roll(top, RWKV_HEAD, 1), bot)


def _chunk_prep(z_rw, shift0, lw, batch, seq):
    chunk = min(SCAN_CHUNK, seq)
    tm = min(ROW_TILE, seq)
    tiles = seq // tm
    cpt = tm // chunk
    n_chunks = seq // chunk
    row = lambda n: pl.BlockSpec((tm, n), lambda b, i: (b * tiles + i, 0))
    per_seq = pl.BlockSpec((None, 1, RWKV_PROJ), lambda b, i: (b, 0, 0))
    consts = _rwkv_consts(lw)
    t = batch * seq
    wide = lambda dt: jax.ShapeDtypeStruct((t, RWKV_WIDTH), dt)
    outs = pl.pallas_call(
        functools.partial(_chunk_prep_kernel, chunk=chunk),
        grid=(batch, tiles),
        in_specs=[row(RWKV_PROJ), per_seq] + [_const_spec(c.shape) for c in consts],
        out_specs=[row(RWKV_WIDTH)] * 10
                  + [pl.BlockSpec((None, cpt, SUBLANES, RWKV_WIDTH), lambda b, i: (b, i, 0, 0)), per_seq],
        out_shape=[wide(BF16)] * 6 + [wide(F32)] * 4
                  + [jax.ShapeDtypeStruct((batch, n_chunks, SUBLANES, RWKV_WIDTH), F32),
                     jax.ShapeDtypeStruct((batch, 1, RWKV_PROJ), F32)],
        scratch_shapes=[pltpu.VMEM((1, RWKV_PROJ), F32)],
        compiler_params=_params("arbitrary", "arbitrary"),
        name="rwkv_chunk_prep",
    )(z_rw, shift0.reshape(batch, 1, RWKV_PROJ), *consts)
    return outs[:10], outs[10], outs[11]


def _chunk_scan_kernel(w_ref, r_ref, bh_ref, kh_ref, v_ref, mrb_ref, u0_ref, y0_ref, g_ref, bonus_ref,
                       elast_ref, s0_ref, lnw_ref, lnb_ref, seg_ref, y_ref, sout_ref, state_sc, *, chunk):
    c = pl.program_id(0)
    nb = w_ref.shape[0]
    hh = RWKV_HEAD
    insts = [(b, p, slice(p * LANES, (p + 1) * LANES)) for b in range(nb) for p in range(RWKV_PAIRS)]

    @pl.when(c == 0)
    def _():
        state_sc[...] = jnp.zeros(state_sc.shape, F32)
        for b, p, _ in insts:
            state_sc[b, p, :hh, :hh] = s0_ref[b, 2 * p]
            state_sc[b, p, hh:, hh:] = s0_ref[b, 2 * p + 1]

    lane = lax.broadcasted_iota(jnp.int32, (chunk, LANES), 1)
    left = lane < hh
    zero = jnp.zeros((chunk, LANES), F32)
    vi = lax.broadcasted_iota(jnp.int32, (LANES, LANES), 0)
    ki = lax.broadcasted_iota(jnp.int32, (LANES, LANES), 1)
    same_head = (vi < hh) == (ki < hh)
    seg = seg_ref[...]

    ps = [_bdot_nt(jnp.concatenate([w_ref[b, :, ls], r_ref[b, :, ls]], axis=0), state_sc[b, p])
          for b, p, ls in insts]
    ys = []
    for (b, p, ls), wr_s in zip(insts, ps):
        u = wr_s[:chunk] + u0_ref[b, :, ls]
        u_bd = jnp.concatenate([jnp.where(left, u, zero), jnp.where(left, zero, u)], axis=0)
        ys.append(wr_s[chunk:] + _bdot(mrb_ref[b, :, ls], u_bd) + y0_ref[b, :, ls])
        uv = jnp.concatenate([u.astype(BF16), v_ref[b, :, ls]], axis=0)
        bk = jnp.concatenate([bh_ref[b, :, ls], kh_ref[b, :, ls]], axis=0)
        upd = _bdot_tn(uv, bk)
        state_sc[b, p] = state_sc[b, p] * elast_ref[b, 0:1, ls] + jnp.where(same_head, upd, 0.0)
    for (b, p, ls), y in zip(insts, ys):
        y_ref[b, :, ls] = _pair_norm_gate(y, seg, lnw_ref[:, ls], lnb_ref[:, ls], bonus_ref[b, :, ls],
                                          g_ref[b, :, ls]).astype(y_ref.dtype)

    @pl.when(c == pl.num_programs(0) - 1)
    def _():
        for b, p, _ in insts:
            sout_ref[b, 2 * p] = state_sc[b, p, :hh, :hh]
            sout_ref[b, 2 * p + 1] = state_sc[b, p, hh:, hh:]


def _chunk_scan(chunk_ops, elast, state0, lw, batch, seq):
    chunk = min(SCAN_CHUNK, seq)
    n_chunks = seq // chunk
    tok = pl.BlockSpec((batch, chunk, RWKV_WIDTH), lambda c: (0, c, 0))
    ops3 = [a.reshape(batch, seq, RWKV_WIDTH) for a in chunk_ops]
    y, s_new = pl.pallas_call(
        functools.partial(_chunk_scan_kernel, chunk=chunk),
        grid=(n_chunks,),
        in_specs=[tok] * 10 + [pl.BlockSpec((batch, SUBLANES, RWKV_WIDTH), lambda c: (0, c, 0)),
                               _const_spec(state0.shape), _const_spec(lw["lnx_w"].shape),
                               _const_spec(lw["lnx_b"].shape), _const_spec(lw["seg_pair"].shape)],
        out_specs=[tok, _const_spec(state0.shape)],
        out_shape=[jax.ShapeDtypeStruct((batch, seq, RWKV_WIDTH), BF16),
                   jax.ShapeDtypeStruct(state0.shape, F32)],
        scratch_shapes=[pltpu.VMEM((batch, RWKV_PAIRS, LANES, LANES), F32)],
        compiler_params=_params("arbitrary"),
        name="rwkv_chunk_scan",
    )(*ops3, elast.reshape(batch, n_chunks * SUBLANES, RWKV_WIDTH), state0, lw["lnx_w"], lw["lnx_b"],
      lw["seg_pair"])
    return y.reshape(batch * seq, RWKV_WIDTH), s_new


def _wkv_step_kernel(r_ref, lw_ref, k_ref, v_ref, an_ref, bn_ref, g_ref, bonus_ref, s_ref, lnw_ref, lnb_ref,
                     y_ref, sout_ref, yraw_sc):
    decay = jnp.exp(lw_ref[...])
    a, b, k, r = an_ref[...], bn_ref[...], k_ref[...], r_ref[...]

    def value_row(vi, carry):
        row = pl.ds(vi, 1)
        state = s_ref[vi]
        sa = jnp.sum(state * a, axis=0, keepdims=True)
        new = state * decay + sa * b + v_ref[row, :] * k
        sout_ref[vi] = new
        yraw_sc[row, :] = jnp.sum(new * r, axis=0, keepdims=True)
        return carry

    lax.fori_loop(0, RWKV_HEAD, value_row, 0, unroll=WKV_VALUE_UNROLL)
    y = yraw_sc[...]
    cen = y - jnp.mean(y, axis=0, keepdims=True)
    var = jnp.mean(cen * cen, axis=0, keepdims=True)
    y_ref[...] = (cen * lax.rsqrt(var + LNX_EPS) * lnw_ref[...] + lnb_ref[...] + bonus_ref[...]) * g_ref[...]


def _wkv_step(layer, prep_t, state_t, lw):
    n_seq = state_t.shape[-1]
    n = RWKV_HEAD
    chan = pl.BlockSpec((n, n_seq), lambda h: (h, 0))
    lnw_t = jnp.broadcast_to(lw["lnx_w"].reshape(RWKV_WIDTH, 1), (RWKV_WIDTH, n_seq))
    lnb_t = jnp.broadcast_to(lw["lnx_b"].reshape(RWKV_WIDTH, 1), (RWKV_WIDTH, n_seq))
    return pl.pallas_call(
        _wkv_step_kernel,
        grid=(RWKV_HEADS,),
        in_specs=[chan] * 8 + [pl.BlockSpec((None, None, n, n, n_seq), lambda h: (layer, h, 0, 0, 0)), chan, chan],
        out_specs=[chan, pl.BlockSpec((None, n, n, n_seq), lambda h: (h, 0, 0, 0))],
        out_shape=[jax.ShapeDtypeStruct((RWKV_WIDTH, n_seq), F32),
                   jax.ShapeDtypeStruct(state_t.shape[1:], F32)],
        scratch_shapes=[pltpu.VMEM((n, n_seq), F32)],
        compiler_params=_params("arbitrary"),
        name="wkv_step",
    )(*prep_t, state_t, lnw_t, lnb_t)


def _merge_kernel(x_ref, ym_ref, yr_ref, gate_ref, gt_ref, gpost_ref, wbm_ref, wbr_ref, wout_ref, o_ref):
    d = x_ref.shape[-1]
    merged = (gate_ref[:, :d] * _bdot(ym_ref[...], wbm_ref[...])
              + gate_ref[:, d:] * _bdot(yr_ref[...], wbr_ref[...]))
    out = _bdot(merged, wout_ref[...])
    o_ref[...] = x_ref[...] + gt_ref[...] * _rms(out, gpost_ref[...])


def _merge(x, y_mla, y_rw, gate, gt, g_post, lw, rows_per_group):
    t, d = x.shape
    tm = min(MERGE_ROW_TILE, rows_per_group)
    tpg = rows_per_group // tm
    row = lambda n: pl.BlockSpec((tm, n), lambda i: (i, 0))
    weights = [lw["w_branch_mla"], lw["w_branch_rwkv"], lw["w_out"]]
    return pl.pallas_call(
        _merge_kernel,
        grid=(t // tm,),
        in_specs=[row(d), row(MLA_WIDTH), row(RWKV_WIDTH), row(2 * d), _mod_spec(gt, tm, tpg),
                  _const_spec(g_post.shape)] + [_const_spec(w.shape) for w in weights],
        out_specs=row(d),
        out_shape=jax.ShapeDtypeStruct((t, d), F32),
        compiler_params=_params("arbitrary"),
        name="merge",
    )(x, y_mla, y_rw, gate, gt, g_post, *weights)


def _ffn_kernel(x_ref, sh_ref, sc_ref, gt_ref, gpre_ref, gpost_ref, w1_ref, w2_ref, o_ref):
    x = x_ref[...]
    hb = (_rms(x, gpre_ref[...]) * (1.0 + sc_ref[...]) + sh_ref[...]).astype(BF16)
    acc = jnp.zeros(x.shape, F32)
    for c0 in range(0, w1_ref.shape[1], FFN_CHUNK):
        a = jnp.maximum(jnp.dot(hb, w1_ref[:, c0:c0 + FFN_CHUNK], preferred_element_type=F32), 0.0)
        acc = acc + _bdot(a * a, w2_ref[c0:c0 + FFN_CHUNK, :])
    o_ref[...] = x + gt_ref[...] * _rms(acc, gpost_ref[...])


def _ffn(layer, x, sh, sc, gt, g_pre, g_post, w_ff1_all, w_ff2_all, rows_per_group):
    t, d = x.shape
    tm = min(FFN_ROW_TILE, rows_per_group)
    tpg = rows_per_group // tm
    row = pl.BlockSpec((tm, d), lambda i: (i, 0))
    of_layer = lambda w: pl.BlockSpec((None,) + w.shape[1:], lambda i: (layer, 0, 0))
    return pl.pallas_call(
        _ffn_kernel,
        grid=(t // tm,),
        in_specs=[row, _mod_spec(sh, tm, tpg), _mod_spec(sc, tm, tpg), _mod_spec(gt, tm, tpg),
                  _const_spec(g_pre.shape), _const_spec(g_post.shape),
                  of_layer(w_ff1_all), of_layer(w_ff2_all)],
        out_specs=row,
        out_shape=jax.ShapeDtypeStruct((t, d), F32),
        compiler_params=_params("arbitrary"),
        name="ffn",
    )(x, sh, sc, gt, g_pre, g_post, w_ff1_all, w_ff2_all)


def _rope_tables(pos):
    inv = ROPE_BASE ** (-jnp.arange(HALF_ROPE, dtype=F32) / HALF_ROPE)
    ang = pos.astype(F32)[:, None] * inv[None, :]
    cos, sin = jnp.cos(ang), jnp.sin(ang)
    n = pos.shape[0]
    ones = jnp.ones((n, ROPE_LANE0), F32)
    zeros = jnp.zeros((n, ROPE_LANE0), F32)
    pad = jnp.zeros((n, LANES - ROPE_LANE0 - QK_ROPE), F32)
    z16 = jnp.zeros((n, HALF_ROPE), F32)
    cos_t = jnp.concatenate([ones, cos, cos, pad], axis=1)
    sin_lo = jnp.concatenate([zeros, -sin, z16, pad], axis=1)
    sin_hi = jnp.concatenate([zeros, z16, sin, pad], axis=1)
    return cos_t, sin_lo, sin_hi


def _layer_weights(l, p):
    w_in = p["w_in"][l]
    w_q = w_in[:, :Q_RANK]
    w_ckv = w_in[:, Q_RANK:Q_RANK + KV_RANK]
    w_kpe = jnp.pad(w_in[:, Q_RANK + KV_RANK:MLA_PROJ], ((0, 0), (ROPE_LANE0, LANES - ROPE_LANE0 - QK_ROPE)))
    w_rw = w_in[:, MLA_PROJ:MLA_PROJ + RWKV_PROJ]
    w_gate = w_in[:, MLA_PROJ + RWKV_PROJ:]
    qk = QK_NOPE + QK_ROPE
    w_uq = jnp.pad(p["w_uq"][l].reshape(Q_RANK, MLA_HEADS, qk), ((0, 0), (0, 0), (0, HEAD_SLAB - qk)))
    w_ukv = p["w_ukv"][l].reshape(KV_RANK, MLA_HEADS, QK_NOPE + V_HEAD)
    w_uk, w_uv = w_ukv[..., :QK_NOPE], w_ukv[..., QK_NOPE:]
    w_ukn = jnp.pad(w_uk, ((0, 0), (0, 0), (0, HEAD_SLAB - QK_NOPE)))
    w_uk_t = jnp.pad(w_uk.transpose(1, 2, 0), ((0, 0), (0, HEAD_SLAB - QK_NOPE), (0, 0)))
    eye = jnp.eye(MLA_HEADS, dtype=F32)
    w_uv_wide = (w_uv.transpose(1, 0, 2)[:, :, None, :] * eye[:, None, :, None]).reshape(
        MLA_HEADS, KV_RANK, MLA_WIDTH)
    lora_pad_lo = ((0, AAA_LORA), (0, 0))
    lora_pad_hi = ((DECAY_LORA, 0), (0, 0))
    heads = jnp.arange(RWKV_WIDTH) // RWKV_HEAD
    seg_ones = (heads[:, None] == heads[None, :]).astype(BF16)
    row2 = lambda a: a.reshape(1, -1)
    return {
        "w_q": w_q.astype(BF16), "w_ckv": w_ckv.astype(BF16), "w_kpe": w_kpe.astype(BF16),
        "w_rw": w_rw.astype(BF16), "w_gate": w_gate.astype(BF16),
        "g_q": row2(p["g_q_norm"][l]), "g_kv": row2(p["g_kv_norm"][l]),
        "w_uq": w_uq.reshape(Q_RANK, MLA_HEADS * HEAD_SLAB).astype(BF16),
        "w_ukn": w_ukn.reshape(KV_RANK, MLA_HEADS * HEAD_SLAB).astype(BF16),
        "w_uv_t": w_uv.reshape(KV_RANK, MLA_WIDTH).T.astype(BF16),
        "w_uk_t": w_uk_t.astype(BF16), "w_uv_wide": w_uv_wide.astype(BF16),
        "mu": row2(p["mu_shift"][l]),
        "rw_w0": row2(p["rw_w0"][l]), "rw_a0": row2(p["rw_a0"][l]),
        "rw_w2": jnp.pad(p["rw_w2"][l], lora_pad_lo).astype(BF16),
        "rw_a2": jnp.pad(p["rw_a2"][l], lora_pad_hi).astype(BF16),
        "rw_g2": p["rw_g2"][l].astype(BF16),
        "rw_k_k": row2(p["rw_k_k"][l]), "rw_k_a": row2(p["rw_k_a"][l]), "rw_r_k": row2(p["rw_r_k"][l]),
        "seg_ones": seg_ones, "seg_pair": seg_ones[:LANES, :LANES],
        "lnx_w": row2(p["rw_lnx_w"][l]), "lnx_b": row2(p["rw_lnx_b"][l]),
        "w_branch_mla": p["w_branch_mla"][l].astype(BF16), "w_branch_rwkv": p["w_branch_rwkv"][l].astype(BF16),
        "w_out": p["w_out"][l].astype(BF16),
        "g_pre_mix": row2(p["g_pre_mix"][l]), "g_post_mix": row2(p["g_post_mix"][l]),
        "g_pre_ff": row2(p["g_pre_ff"][l]), "g_post_ff": row2(p["g_post_ff"][l]),
    }


def _trunk(x, mods, rope_tabs, rows_per_group, layer_weights, w_ff1_all, w_ff2_all, mixers):
    ckv_rows, kpe_rows, wkv_states, shift_rows = [], [], [], []
    for l in range(DEPTH):
        lw = layer_weights[l]
        sh1, sc1, gt1, sh2, sc2, gt2 = mods[l]
        q, k, vt, ckv, kpe, z_rw, gate = _inproj(x, sh1, sc1, lw["g_pre_mix"], rope_tabs, lw, rows_per_group)
        y_mla, y_rw, wkv_new, shift_new = mixers(l, lw, q, k, vt, ckv, kpe, z_rw)
        x = _merge(x, y_mla, y_rw, gate, gt1, lw["g_post_mix"], lw, rows_per_group)
        x = _ffn(l, x, sh2, sc2, gt2, lw["g_pre_ff"], lw["g_post_ff"], w_ff1_all, w_ff2_all, rows_per_group)
        ckv_rows.append(ckv)
        kpe_rows.append(kpe)
        wkv_states.append(wkv_new)
        shift_rows.append(shift_new)
    return x, jnp.stack(ckv_rows), jnp.stack(kpe_rows), jnp.stack(wkv_states), jnp.stack(shift_rows)


def kernel(x_prompt, x_sample, c_prompt, c_sample, cache_ckv, cache_kpe, state_wkv, state_shift, page_table, w_ada, b_ada, g_pre_mix, g_post_mix, g_pre_ff, g_post_ff, w_in, mu_shift, g_q_norm, w_uq, g_kv_norm, w_ukv, rw_w0, rw_w2, rw_a0, rw_a2, rw_g2, rw_k_k, rw_k_a, rw_r_k, rw_lnx_w, rw_lnx_b, w_branch_mla, w_branch_rwkv, w_out, w_ff1, w_ff2):
    params = dict(g_pre_mix=g_pre_mix, g_post_mix=g_post_mix, g_pre_ff=g_pre_ff, g_post_ff=g_post_ff,
                  w_in=w_in, mu_shift=mu_shift, g_q_norm=g_q_norm, w_uq=w_uq, g_kv_norm=g_kv_norm,
                  w_ukv=w_ukv, rw_w0=rw_w0, rw_w2=rw_w2, rw_a0=rw_a0, rw_a2=rw_a2, rw_g2=rw_g2,
                  rw_k_k=rw_k_k, rw_k_a=rw_k_a, rw_r_k=rw_r_k, rw_lnx_w=rw_lnx_w, rw_lnx_b=rw_lnx_b,
                  w_branch_mla=w_branch_mla, w_branch_rwkv=w_branch_rwkv, w_out=w_out,
                  w_ff1=w_ff1, w_ff2=w_ff2)
    b_p, s_p, d = x_prompt.shape
    b_s, s_s, _ = x_sample.shape
    assert s_s == 1, "the sample group advances one token per sequence"
    past_len = page_table.shape[1] * cache_ckv.shape[2]
    layer_weights = [_layer_weights(l, params) for l in range(DEPTH)]
    w_ff1_all, w_ff2_all = w_ff1.astype(BF16), w_ff2.astype(BF16)

    mod = _adaln(jnp.concatenate([c_prompt, c_sample], axis=0), w_ada, b_ada)
    mod = mod.reshape(DEPTH, b_p + b_s, 6, d)
    mods_p = [[mod[l, :b_p, j].reshape(b_p, 1, d) for j in range(6)] for l in range(DEPTH)]
    mods_s = [[mod[l, b_p:, j].reshape(1, b_s, d) for j in range(6)] for l in range(DEPTH)]

    shift0_p = jnp.zeros((b_p, RWKV_PROJ), F32)
    wkv0_p = jnp.zeros((b_p, RWKV_HEADS, RWKV_HEAD, RWKV_HEAD), F32)

    def prompt_mixers(l, lw, q, k, vt, ckv, kpe, z_rw):
        y_mla = _prompt_attention(q, k, vt, b_p, s_p)
        chunk_ops, elast, last = _chunk_prep(z_rw, shift0_p, lw, b_p, s_p)
        y_rw, wkv_new = _chunk_scan(chunk_ops, elast, wkv0_p, lw, b_p, s_p)
        return y_mla, y_rw, wkv_new, last.reshape(b_p, RWKV_PROJ)

    y_p, p_ckv, p_kpe, p_wkv, p_shift = _trunk(
        x_prompt.reshape(b_p * s_p, d), mods_p, _rope_tables(jnp.arange(s_p)), s_p, layer_weights,
        w_ff1_all, w_ff2_all, prompt_mixers)

    cache_kpe_t = cache_kpe.transpose(0, 1, 3, 2)
    state_wkv_t = state_wkv.transpose(0, 2, 3, 4, 1)

    def sample_mixers(l, lw, q, k, vt, ckv, kpe, z_rw):
        y_mla = _sample_attention(l, q, ckv, kpe, cache_ckv, cache_kpe_t, page_table, lw)
        prep_t = _token_prep(z_rw, state_shift[l], lw)
        y_rw_t, wkv_new_t = _wkv_step(l, prep_t, state_wkv_t, lw)
        return y_mla, y_rw_t.T, wkv_new_t, z_rw

    y_s, s_ckv, s_kpe, s_wkv_t, s_shift = _trunk(
        x_sample.reshape(b_s, d), mods_s, _rope_tables(jnp.full((b_s,), past_len)), b_s, layer_weights,
        w_ff1_all, w_ff2_all, sample_mixers)
    s_wkv = s_wkv_t.transpose(0, 4, 1, 2, 3)

    return (y_p.reshape(b_p, s_p, d), y_s.reshape(b_s, 1, d),
            p_ckv.reshape(DEPTH, b_p, s_p, KV_RANK), p_kpe.reshape(DEPTH, b_p, s_p, QK_ROPE), p_wkv, p_shift,
            s_ckv.reshape(DEPTH, b_s, 1, KV_RANK), s_kpe.reshape(DEPTH, b_s, 1, QK_ROPE), s_wkv, s_shift)
```

```python
import functools

import jax
import jax.numpy as jnp
from jax import lax
from jax.experimental import pallas as pl
from jax.experimental.pallas import tpu as pltpu

F32 = jnp.float32
BF16 = jnp.bfloat16

D_MODEL = 1024
DEPTH = 4
MLA_HEADS = 8
QK_NOPE = 64
QK_ROPE = 32
V_HEAD = 64
Q_RANK = 384
KV_RANK = 256
MLA_WIDTH = MLA_HEADS * V_HEAD
MLA_PROJ = Q_RANK + KV_RANK + QK_ROPE
ATTN_SCALE = (QK_NOPE + QK_ROPE) ** -0.5
LOG2_E = 1.4426950408889634
ROPE_BASE = 10000.0
RWKV_HEADS = 8
RWKV_HEAD = 64
RWKV_WIDTH = RWKV_HEADS * RWKV_HEAD
RWKV_PAIRS = RWKV_HEADS // 2
DECAY_LORA = 64
AAA_LORA = 64
GATE_LORA = 128
RWKV_PROJ = 3 * RWKV_WIDTH + DECAY_LORA + AAA_LORA + GATE_LORA
LNX_EPS = 64e-5
D_FF = 4 * D_MODEL
NORM_EPS = 1e-6

LANES = 128
SUBLANES = 8
HEAD_SLAB = LANES
ROPE_LANE0 = QK_NOPE
HALF_ROPE = QK_ROPE // 2
VMEM_LIMIT_BYTES = 56 * 1024 * 1024

ROW_TILE = 256
INPROJ_ROW_TILE = 256
MERGE_ROW_TILE = 512
FFN_ROW_TILE = 512
FFN_CHUNK = 1024
ATTN_TILE = 512
SCAN_CHUNK = 64
DECODE_SLOTS = 2
DECODE_DMA_UNROLL = 8
DECODE_KEY_SPLITS = 4
WKV_VALUE_UNROLL = 8


def _params(*sem):
    return pltpu.CompilerParams(dimension_semantics=sem, vmem_limit_bytes=VMEM_LIMIT_BYTES)


def _bdot(a, b):
    return jnp.dot(a.astype(BF16), b.astype(BF16), preferred_element_type=F32)


def _bdot_nt(a, b):
    return lax.dot_general(a.astype(BF16), b.astype(BF16), (((1,), (1,)), ((), ())),
                           preferred_element_type=F32)


def _bdot_tn(a, b):
    return lax.dot_general(a.astype(BF16), b.astype(BF16), (((0,), (0,)), ((), ())),
                           preferred_element_type=F32)


def _rms(x, g):
    return x * lax.rsqrt(jnp.mean(x * x, axis=-1, keepdims=True) + NORM_EPS) * g


def _split_dot(x, ones_rhs, parts):
    acc = None
    rem = x
    for _ in range(parts):
        piece = rem.astype(BF16)
        rem = rem - piece.astype(F32)
        d = jnp.dot(piece, ones_rhs, preferred_element_type=F32)
        acc = d if acc is None else acc + d
    return acc


def _ones_split_dot(ones_lhs, x, parts):
    acc = None
    rem = x
    for _ in range(parts):
        piece = rem.astype(BF16)
        rem = rem - piece.astype(F32)
        d = jnp.dot(ones_lhs, piece, preferred_element_type=F32)
        acc = d if acc is None else acc + d
    return acc


def _const_spec(shape):
    nd = len(shape)
    return pl.BlockSpec(shape, lambda *_: (0,) * nd)


def _mod_spec(mod, tm, tiles_per_group):
    _, r, d = mod.shape
    if r == 1:
        return pl.BlockSpec((None, 1, d), lambda i: (i // tiles_per_group, 0, 0))
    return pl.BlockSpec((None, tm, d), lambda i: (i // tiles_per_group, i % tiles_per_group, 0))


def _ada_kernel(c_ref, w_ref, b_ref, o_ref):
    c = c_ref[...]
    o_ref[...] = _bdot(c * jax.nn.sigmoid(c), w_ref[...]) + b_ref[...]


def _adaln(c_all, w_ada, b_ada):
    depth, d, n = w_ada.shape
    rows = c_all.shape[0]
    tn = n // 4
    return pl.pallas_call(
        _ada_kernel,
        grid=(depth, n // tn),
        in_specs=[pl.BlockSpec((rows, d), lambda l, j: (0, 0)),
                  pl.BlockSpec((None, d, tn), lambda l, j: (l, 0, j)),
                  pl.BlockSpec((None, 1, tn), lambda l, j: (l, 0, j))],
        out_specs=pl.BlockSpec((None, rows, tn), lambda l, j: (l, 0, j)),
        out_shape=jax.ShapeDtypeStruct((depth, rows, n), F32),
        compiler_params=_params("arbitrary", "arbitrary"),
        name="adaln",
    )(c_all, w_ada, b_ada.reshape(depth, 1, n))


def _rope_slab(x, cos, sin_lo, sin_hi):
    return (x * cos + pltpu.roll(x, LANES - HALF_ROPE, 1) * sin_lo
            + pltpu.roll(x, HALF_ROPE, 1) * sin_hi)


def _inproj_kernel(x_ref, sh_ref, sc_ref, gpre_ref, cos_ref, slo_ref, shi_ref,
                   wq_ref, wckv_ref, wkpe_ref, wrw_ref, wgate_ref, gq_ref, wuq_ref,
                   gkv_ref, wukn_ref, wuvt_ref,
                   q_ref, k_ref, vt_ref, ckv_ref, kpe_ref, zrw_ref, gate_ref):
    x = x_ref[...]
    h = _rms(x, gpre_ref[...]) * (1.0 + sc_ref[...]) + sh_ref[...]
    hb = h.astype(BF16)
    zrw_ref[...] = jnp.dot(hb, wrw_ref[...], preferred_element_type=F32)
    gate_ref[...] = jax.nn.sigmoid(jnp.dot(hb, wgate_ref[...], preferred_element_type=F32))
    cos, slo, shi = cos_ref[...], slo_ref[...], shi_ref[...]

    q_lat = jnp.dot(hb, wq_ref[...], preferred_element_type=F32)
    q = _bdot(_rms(q_lat, gq_ref[...]), wuq_ref[...])
    for hd in range(MLA_HEADS):
        sl = slice(hd * HEAD_SLAB, (hd + 1) * HEAD_SLAB)
        q_ref[:, sl] = _rope_slab(q[:, sl], cos, slo, shi).astype(BF16)

    ckv = _rms(jnp.dot(hb, wckv_ref[...], preferred_element_type=F32), gkv_ref[...])
    ckv_ref[...] = ckv
    cb = ckv.astype(BF16)
    kpe = _rope_slab(jnp.dot(hb, wkpe_ref[...], preferred_element_type=F32), cos, slo, shi)
    kpe_ref[...] = kpe[:, ROPE_LANE0:ROPE_LANE0 + QK_ROPE]
    kn = jnp.dot(cb, wukn_ref[...], preferred_element_type=F32)
    for hd in range(MLA_HEADS):
        sl = slice(hd * HEAD_SLAB, (hd + 1) * HEAD_SLAB)
        k_ref[:, sl] = (kn[:, sl] + kpe).astype(BF16)
    vt_ref[...] = _bdot_nt(wuvt_ref[...], cb).astype(BF16)


def _inproj(x, sh, sc, g_pre, rope_tabs, lw, rows_per_group):
    t, d = x.shape
    tm = min(INPROJ_ROW_TILE, rows_per_group)
    tpg = rows_per_group // tm
    cos, slo, shi = rope_tabs
    tab_spec = pl.BlockSpec((tm, LANES), lambda i: (i % tpg, 0))
    row = lambda n: pl.BlockSpec((tm, n), lambda i: (i, 0))
    weights = [lw["w_q"], lw["w_ckv"], lw["w_kpe"], lw["w_rw"], lw["w_gate"], lw["g_q"],
               lw["w_uq"], lw["g_kv"], lw["w_ukn"], lw["w_uv_t"]]
    slab = MLA_HEADS * HEAD_SLAB
    vt_spec = pl.BlockSpec((None, MLA_WIDTH, tm), lambda i: (i // tpg, 0, i % tpg))
    return pl.pallas_call(
        _inproj_kernel,
        grid=(t // tm,),
        in_specs=[row(d), _mod_spec(sh, tm, tpg), _mod_spec(sc, tm, tpg), _const_spec(g_pre.shape),
                  tab_spec, tab_spec, tab_spec] + [_const_spec(w.shape) for w in weights],
        out_specs=[row(slab), row(slab), vt_spec, row(KV_RANK), row(QK_ROPE),
                   row(RWKV_PROJ), row(2 * d)],
        out_shape=[jax.ShapeDtypeStruct((t, slab), BF16), jax.ShapeDtypeStruct((t, slab), BF16),
                   jax.ShapeDtypeStruct((t // rows_per_group, MLA_WIDTH, rows_per_group), BF16),
                   jax.ShapeDtypeStruct((t, KV_RANK), F32),
                   jax.ShapeDtypeStruct((t, QK_ROPE), F32), jax.ShapeDtypeStruct((t, RWKV_PROJ), F32),
                   jax.ShapeDtypeStruct((t, 2 * d), F32)],
        compiler_params=_params("arbitrary"),
        name="inproj",
    )(x, sh, sc, g_pre, cos, slo, shi, *weights)


def _attn_block(q_ref, k_ref, vt_ref, m_sc, l_sc, acc_sc, valid):
    def scores(hd):
        sl = slice(hd * HEAD_SLAB, (hd + 1) * HEAD_SLAB)
        return _bdot_nt(k_ref[:, sl], q_ref[:, sl])

    ahead = scores(0)
    for hd in range(MLA_HEADS):
        rows = slice(hd * V_HEAD, (hd + 1) * V_HEAD)
        s = ahead * (ATTN_SCALE * LOG2_E)
        if hd + 1 < MLA_HEADS:
            ahead = scores(hd + 1)
        if valid is not None:
            s = jnp.where(valid, s, -jnp.inf)
        m_prev = m_sc[hd]
        m_new = jnp.maximum(m_prev, jnp.max(s, axis=0, keepdims=True))
        alpha = jnp.exp2(m_prev - m_new)
        p = jnp.exp2(s - m_new)
        l_sc[hd] = alpha * l_sc[hd] + jnp.sum(p, axis=0, keepdims=True)
        m_sc[hd] = m_new
        acc_sc[rows, :] = alpha * acc_sc[rows, :] + jnp.dot(vt_ref[rows, :], p.astype(BF16),
                                                           preferred_element_type=F32)


def _attn_kernel(q_ref, k_ref, vt_ref, o_ref, m_sc, l_sc, acc_sc, *, tile):
    i = pl.program_id(1)
    j = pl.program_id(2)

    @pl.when(j == 0)
    def _():
        m_sc[...] = jnp.full(m_sc.shape, -jnp.inf, F32)
        l_sc[...] = jnp.zeros(l_sc.shape, F32)
        acc_sc[...] = jnp.zeros(acc_sc.shape, F32)

    @pl.when(j < i)
    def _():
        _attn_block(q_ref, k_ref, vt_ref, m_sc, l_sc, acc_sc, None)

    @pl.when(j == i)
    def _():
        key = lax.broadcasted_iota(jnp.int32, (tile, tile), 0)
        qry = lax.broadcasted_iota(jnp.int32, (tile, tile), 1)
        _attn_block(q_ref, k_ref, vt_ref, m_sc, l_sc, acc_sc, key <= qry)
        for hd in range(MLA_HEADS):
            rows = slice(hd * V_HEAD, (hd + 1) * V_HEAD)
            acc_sc[rows, :] = acc_sc[rows, :] * (1.0 / l_sc[hd])
        o_ref[...] = acc_sc[...].T.astype(o_ref.dtype)


def _prompt_attention(q, k, vt, batch, seq):
    tile = min(ATTN_TILE, seq)
    nt = seq // tile
    slab = MLA_HEADS * HEAD_SLAB
    q3, k3 = (a.reshape(batch, seq, slab) for a in (q, k))
    out = pl.pallas_call(
        functools.partial(_attn_kernel, tile=tile),
        grid=(batch, nt, nt),
        in_specs=[pl.BlockSpec((None, tile, slab), lambda b, i, j: (b, i, 0)),
                  pl.BlockSpec((None, tile, slab), lambda b, i, j: (b, jnp.minimum(i, j), 0)),
                  pl.BlockSpec((None, MLA_WIDTH, tile), lambda b, i, j: (b, 0, jnp.minimum(i, j)))],
        out_specs=pl.BlockSpec((None, tile, MLA_WIDTH), lambda b, i, j: (b, i, 0)),
        out_shape=jax.ShapeDtypeStruct((batch, seq, MLA_WIDTH), BF16),
        scratch_shapes=[pltpu.VMEM((MLA_HEADS, 1, tile), F32), pltpu.VMEM((MLA_HEADS, 1, tile), F32),
                        pltpu.VMEM((MLA_WIDTH, tile), F32)],
        compiler_params=_params("arbitrary", "arbitrary", "arbitrary"),
        name="prompt_attention",
    )(q3, k3, vt)
    return out.reshape(batch * seq, MLA_WIDTH)


def _qlat_kernel(q_ref, wuk_ref, o_ref):
    for hd in range(MLA_HEADS):
        o_ref[hd] = jnp.dot(q_ref[:, hd * HEAD_SLAB:(hd + 1) * HEAD_SLAB], wuk_ref[hd],
                            preferred_element_type=F32)


def _decode_kernel(pt_ref, qlat_ref, qlt_ref, qpe_ref, cnew_ref, knew_ref, ckv_hbm, kpet_hbm, o_ref,
                   ckv_buf, kpet_buf, sems, *, layer, n_pages, page):
    b = pl.program_id(0)
    nb = pl.num_programs(0)

    def page_copies(seq, slot, pg):
        pid = pt_ref[seq * n_pages + pg]
        rows = pl.ds(pl.multiple_of(pg * page, page), page)
        return (pltpu.make_async_copy(ckv_hbm.at[layer, pid], ckv_buf.at[slot, rows, :], sems.at[0, slot]),
                pltpu.make_async_copy(kpet_hbm.at[layer, pid], kpet_buf.at[slot, :, rows], sems.at[1, slot]))

    def start_gather(seq, slot):
        def body(pg, carry):
            for queue, cp in enumerate(page_copies(seq, slot, pg)):
                cp.start(priority=queue)
            return carry
        lax.fori_loop(0, n_pages, body, 0, unroll=DECODE_DMA_UNROLL)

    def wait_gather(seq, slot):
        def body(pg, carry):
            for cp in page_copies(seq, slot, pg):
                cp.wait()
            return carry
        lax.fori_loop(0, n_pages, body, 0, unroll=DECODE_DMA_UNROLL)

    slot = b % DECODE_SLOTS

    @pl.when(b == 0)
    def _():
        start_gather(b, slot)

    @pl.when(b + 1 < nb)
    def _():
        start_gather(b + 1, (b + 1) % DECODE_SLOTS)

    wait_gather(b, slot)

    ql, qp = qlat_ref[...], qpe_ref[...]
    cn, kn = cnew_ref[...], knew_ref[...]
    qpb = qp.astype(BF16)
    qlt = qlt_ref[...]
    m_run = (jnp.sum(ql * cn, axis=-1, keepdims=True)
             + jnp.sum(qp * kn, axis=-1, keepdims=True)) * ATTN_SCALE
    l_run = jnp.ones_like(m_run)
    acc = jnp.broadcast_to(cn, (MLA_HEADS, KV_RANK))
    span = n_pages * page // DECODE_KEY_SPLITS

    def scores(i):
        sp = pl.ds(i * span, span)
        ck = ckv_buf[slot, sp, :].astype(BF16)
        s_t = jnp.dot(ck, qlt, preferred_element_type=F32)
        return ck, (s_t.T[:MLA_HEADS] + _bdot(qpb, kpet_buf[slot, :, sp])) * ATTN_SCALE

    ahead = scores(0)
    for i in range(DECODE_KEY_SPLITS):
        ck, s = ahead
        if i + 1 < DECODE_KEY_SPLITS:
            ahead = scores(i + 1)
        m_new = jnp.maximum(m_run, jnp.max(s, axis=-1, keepdims=True))
        alpha = jnp.exp(m_run - m_new)
        p = jnp.exp(s - m_new)
        l_run = alpha * l_run + jnp.sum(p, axis=-1, keepdims=True)
        acc = alpha * acc + jnp.dot(p.astype(BF16), ck, preferred_element_type=F32)
        m_run = m_new
    o_ref[...] = acc / l_run


def _olat_kernel(o_ref, wuv_ref, y_ref):
    acc = None
    for hd in range(MLA_HEADS):
        d = _bdot(o_ref[hd], wuv_ref[hd])
        acc = d if acc is None else acc + d
    y_ref[...] = acc.astype(y_ref.dtype)


def _sample_attention(layer, q_slab, ckv_new, kpe_new, cache_ckv, cache_kpe_t, page_table, lw):
    b = q_slab.shape[0]
    n_pages = page_table.shape[1]
    page = cache_ckv.shape[2]
    past_len = n_pages * page
    qlat = pl.pallas_call(
        _qlat_kernel,
        out_shape=jax.ShapeDtypeStruct((MLA_HEADS, b, KV_RANK), F32),
        compiler_params=pltpu.CompilerParams(vmem_limit_bytes=VMEM_LIMIT_BYTES),
        name="decode_qlat",
    )(q_slab, lw["w_uk_t"])
    qlat_t = jnp.pad(qlat.transpose(1, 2, 0), ((0, 0), (0, 0), (0, LANES - MLA_HEADS))).astype(BF16)
    qlat = qlat.transpose(1, 0, 2)
    qpe = q_slab.reshape(b, MLA_HEADS, HEAD_SLAB)[:, :, ROPE_LANE0:ROPE_LANE0 + QK_ROPE].astype(F32)

    per_seq = lambda shape: pl.BlockSpec((None,) + shape, lambda bi, pt: (bi, 0, 0))
    grid_spec = pltpu.PrefetchScalarGridSpec(
        num_scalar_prefetch=1,
        grid=(b,),
        in_specs=[per_seq((MLA_HEADS, KV_RANK)), per_seq((KV_RANK, LANES)), per_seq((MLA_HEADS, QK_ROPE)),
                  per_seq((1, KV_RANK)), per_seq((1, QK_ROPE)),
                  pl.BlockSpec(memory_space=pl.ANY), pl.BlockSpec(memory_space=pl.ANY)],
        out_specs=per_seq((MLA_HEADS, KV_RANK)),
        scratch_shapes=[pltpu.VMEM((DECODE_SLOTS, past_len, KV_RANK), F32),
                        pltpu.VMEM((DECODE_SLOTS, QK_ROPE, past_len), F32),
                        pltpu.SemaphoreType.DMA((2, DECODE_SLOTS))],
    )
    olat = pl.pallas_call(
        functools.partial(_decode_kernel, layer=layer, n_pages=n_pages, page=page),
        grid_spec=grid_spec,
        out_shape=jax.ShapeDtypeStruct((b, MLA_HEADS, KV_RANK), F32),
        compiler_params=_params("arbitrary"),
        name="decode_attention",
    )(page_table.reshape(-1), qlat, qlat_t, qpe, ckv_new.reshape(b, 1, KV_RANK), kpe_new.reshape(b, 1, QK_ROPE),
      cache_ckv, cache_kpe_t)
    return pl.pallas_call(
        _olat_kernel,
        out_shape=jax.ShapeDtypeStruct((b, MLA_WIDTH), BF16),
        compiler_params=pltpu.CompilerParams(vmem_limit_bytes=VMEM_LIMIT_BYTES),
        name="decode_out",
    )(olat.transpose(1, 0, 2), lw["w_uv_wide"])


def _softplus(x):
    return jnp.maximum(x, 0.0) + jnp.log(1.0 + jnp.exp(-jnp.abs(x)))


def _rwkv_token_ops(zm, w0, w2, a0, a2, g2, k_k, k_a, r_k, seg):
    w = RWKV_WIDTH
    r, k, v = zm[:, :w], zm[:, w:2 * w], zm[:, 2 * w:3 * w]
    xwa = zm[:, 3 * w:3 * w + DECAY_LORA + AAA_LORA]
    xg = zm[:, 3 * w + DECAY_LORA + AAA_LORA:]
    w_log = -_softplus(-(w0 + _bdot(jnp.tanh(xwa), w2))) - 0.5
    a = jax.nn.sigmoid(a0 + _bdot(xwa, a2))
    g = _bdot(jax.nn.sigmoid(xg), g2)
    kk = k * k_k
    kk = kk / jnp.maximum(jnp.sqrt(_split_dot(kk * kk, seg, 2)), 1e-12)
    k2 = k * (1.0 + (a - 1.0) * k_a)
    bonus = _split_dot(r * k2 * r_k, seg, 2) * v
    return r, -jnp.exp(w_log), k2, v, -kk, kk * a, g, bonus


def _rwkv_consts(lw):
    return [lw["mu"], lw["rw_w0"], lw["rw_w2"], lw["rw_a0"], lw["rw_a2"], lw["rw_g2"], lw["rw_k_k"],
            lw["rw_k_a"], lw["rw_r_k"], lw["seg_ones"]]


def _token_prep_kernel(z_ref, zprev_ref, mu_ref, w0_ref, w2_ref, a0_ref, a2_ref, g2_ref, kk_ref, ka_ref,
                       rk_ref, seg_ref, *out_refs):
    z = z_ref[...]
    zm = z + mu_ref[...] * (zprev_ref[...] - z)
    outs = _rwkv_token_ops(zm, w0_ref[...], w2_ref[...], a0_ref[...], a2_ref[...], g2_ref[...],
                           kk_ref[...], ka_ref[...], rk_ref[...], seg_ref[...])
    for o_ref, val in zip(out_refs, outs):
        o_ref[...] = val.T


def _token_prep(z_rw, z_prev, lw):
    t = z_rw.shape[0]
    tm = min(ROW_TILE, t)
    row = lambda n: pl.BlockSpec((tm, n), lambda i: (i, 0))
    consts = _rwkv_consts(lw)
    return pl.pallas_call(
        _token_prep_kernel,
        grid=(t // tm,),
        in_specs=[row(RWKV_PROJ), row(RWKV_PROJ)] + [_const_spec(c.shape) for c in consts],
        out_specs=[pl.BlockSpec((RWKV_WIDTH, tm), lambda i: (0, i))] * 8,
        out_shape=[jax.ShapeDtypeStruct((RWKV_WIDTH, t), F32)] * 8,
        compiler_params=_params("arbitrary"),
        name="rwkv_token_prep",
    )(z_rw, z_prev, *consts)


def _pair_norm_gate(y, seg_pair, lnw, lnb, bonus, g):
    inv_n = 1.0 / RWKV_HEAD
    cen = y - _split_dot(y, seg_pair, 2) * inv_n
    var = _split_dot(cen * cen, seg_pair, 2) * inv_n
    return (cen * lax.rsqrt(var + LNX_EPS) * lnw + lnb + bonus) * g


def _chunk_prep_kernel(z_ref, s0_ref, mu_ref, w0_ref, w2_ref, a0_ref, a2_ref, g2_ref, kk_ref, ka_ref,
                       rk_ref, seg_ref,
                       w_ref, r_ref, bh_ref, kh_ref, v_ref, mrb_ref, u0_ref, y0_ref, g_ref, bonus_ref,
                       elast_ref, last_ref, carry_sc, *, chunk):
    i = pl.program_id(1)
    z = z_ref[...]
    rows = z.shape[0]
    first = jnp.where(i == 0, s0_ref[...], carry_sc[...])
    is_row0 = lax.broadcasted_iota(jnp.int32, z.shape, 0) == 0
    z_prev = jnp.where(is_row0, first, pltpu.roll(z, 1, 0))
    carry_sc[...] = z[rows - 1:rows, :]
    last_ref[...] = z[rows - 1:rows, :]
    zm = z + mu_ref[...] * (z_prev - z)
    r, lw, k2, v, an, bn, g, bonus = _rwkv_token_ops(
        zm, w0_ref[...], w2_ref[...], a0_ref[...], a2_ref[...], g2_ref[...], kk_ref[...], ka_ref[...],
        rk_ref[...], seg_ref[...])
    g_ref[...] = g
    bonus_ref[...] = bonus
    v_ref[...] = v.astype(BF16)

    ri = lax.broadcasted_iota(jnp.int32, (rows, rows), 0)
    ci = lax.broadcasted_iota(jnp.int32, (rows, rows), 1)
    chunk_start = ri - jnp.bitwise_and(ri, chunk - 1)
    tri = jnp.where((ci <= ri) & (ci >= chunk_start), 1.0, 0.0).astype(BF16)
    cum = _ones_split_dot(tri, lw, 3)
    n_chunks = rows // chunk
    lasts = [cum[(c + 1) * chunk - 1:(c + 1) * chunk, :] for c in range(n_chunks)]
    cum_last = jnp.concatenate([jnp.broadcast_to(x, (chunk, RWKV_WIDTH)) for x in lasts], axis=0)
    for c in range(n_chunks):
        elast_ref[c] = jnp.broadcast_to(jnp.exp(lasts[c]), (SUBLANES, RWKV_WIDTH))
    e_neg = jnp.exp(-cum)
    e_tail = jnp.exp(cum_last - cum)
    a_t = an * jnp.exp(cum - lw)
    r_t = r * jnp.exp(cum)
    b_t, k_t = bn * e_neg, k2 * e_neg
    r_ref[...] = r_t.astype(BF16)
    bh_ref[...] = (bn * e_tail).astype(BF16)
    kh_ref[...] = (k2 * e_tail).astype(BF16)

    lane = lax.broadcasted_iota(jnp.int32, (chunk, LANES), 1)
    tok = lax.broadcasted_iota(jnp.int32, (chunk, LANES), 0)
    left = lane < RWKV_HEAD
    col = jnp.bitwise_and(lane, RWKV_HEAD - 1)
    strict, incl = col < tok, col <= tok
    zero = jnp.zeros((chunk, LANES), F32)
    insts = [(slice(c * chunk, (c + 1) * chunk), slice(p * LANES, (p + 1) * LANES))
             for c in range(n_chunks) for p in range(RWKV_PAIRS)]

    prods = []
    for rs, ls in insts:
        ar = jnp.concatenate([a_t[rs, ls], r_t[rs, ls]], axis=0)
        bp, kp = b_t[rs, ls], k_t[rs, ls]
        bk = jnp.concatenate([jnp.where(left, bp, zero), jnp.where(left, zero, bp),
                              jnp.where(left, kp, zero), jnp.where(left, zero, kp)], axis=0)
        prods.append(_bdot_nt(ar, bk))
    l_pows, sols = [], []
    for (rs, ls), pq in zip(insts, prods):
        ab, ak = pq[:chunk, :LANES], pq[:chunk, LANES:]
        rb, rk = pq[chunk:, :LANES], pq[chunk:, LANES:]
        mrb_ref[rs, ls] = jnp.where(incl, rb, 0.0).astype(BF16)
        vp = v[rs, ls]
        v_bd = jnp.concatenate([jnp.where(left, vp, zero), jnp.where(left, zero, vp)], axis=0)
        lm = jnp.concatenate([jnp.where(strict, ak, 0.0), jnp.where(incl, rk, 0.0)], axis=0)
        lmv = _bdot(lm, v_bd)
        y0_ref[rs, ls] = lmv[chunk:]
        lv = lmv[:chunk]
        ap = a_t[rs, ls]
        sols.append(jnp.concatenate([jnp.where(left, ap, pltpu.roll(lv, RWKV_HEAD, 1)),
                                     jnp.where(left, pltpu.roll(ap, RWKV_HEAD, 1), lv)], axis=0))
        l_pows.append(jnp.concatenate([jnp.where(strict & left, ab, 0.0),
                                       jnp.where(strict & jnp.logical_not(left), ab, 0.0)], axis=0))
    span = 1
    while span < chunk:
        sols = [x + _bdot(lp, x) for lp, x in zip(l_pows, sols)]
        span *= 2
        if span < chunk:
            l_pows = [_bdot(lp, lp) for lp in l_pows]
    for (rs, ls), x in zip(insts, sols):
        top, bot = x[:chunk], x[chunk:]
        w_ref[rs, ls] = jnp.where(left, top, pltpu.roll(bot, RWKV_HEAD, 1)).astype(BF16)
        u0_ref[rs, ls] = jnp.where(left, pltpu.roll(top, RWKV_HEAD, 1), bot)


def _chunk_prep(z_rw, shift0, lw, batch, seq):
    chunk = min(SCAN_CHUNK, seq)
    tm = min(ROW_TILE, seq)
    tiles = seq // tm
    cpt = tm // chunk
    n_chunks = seq // chunk
    row = lambda n: pl.BlockSpec((tm, n), lambda b, i: (b * tiles + i, 0))
    per_seq = pl.BlockSpec((None, 1, RWKV_PROJ), lambda b, i: (b, 0, 0))
    consts = _rwkv_consts(lw)
    t = batch * seq
    wide = lambda dt: jax.ShapeDtypeStruct((t, RWKV_WIDTH), dt)
    outs = pl.pallas_call(
        functools.partial(_chunk_prep_kernel, chunk=chunk),
        grid=(batch, tiles),
        in_specs=[row(RWKV_PROJ), per_seq] + [_const_spec(c.shape) for c in consts],
        out_specs=[row(RWKV_WIDTH)] * 10
                  + [pl.BlockSpec((None, cpt, SUBLANES, RWKV_WIDTH), lambda b, i: (b, i, 0, 0)), per_seq],
        out_shape=[wide(BF16)] * 6 + [wide(F32)] * 4
                  + [jax.ShapeDtypeStruct((batch, n_chunks, SUBLANES, RWKV_WIDTH), F32),
                     jax.ShapeDtypeStruct((batch, 1, RWKV_PROJ), F32)],
        scratch_shapes=[pltpu.VMEM((1, RWKV_PROJ), F32)],
        compiler_params=_params("arbitrary", "arbitrary"),
        name="rwkv_chunk_prep",
    )(z_rw, shift0.reshape(batch, 1, RWKV_PROJ), *consts)
    return outs[:10], outs[10], outs[11]


def _chunk_scan_kernel(w_ref, r_ref, bh_ref, kh_ref, v_ref, mrb_ref, u0_ref, y0_ref, g_ref, bonus_ref,
                       elast_ref, s0_ref, lnw_ref, lnb_ref, seg_ref, y_ref, sout_ref, state_sc, *, chunk):
    c = pl.program_id(0)
    nb = w_ref.shape[0]
    hh = RWKV_HEAD
    insts = [(b, p, slice(p * LANES, (p + 1) * LANES)) for b in range(nb) for p in range(RWKV_PAIRS)]

    @pl.when(c == 0)
    def _():
        state_sc[...] = jnp.zeros(state_sc.shape, F32)
        for b, p, _ in insts:
            state_sc[b, p, :hh, :hh] = s0_ref[b, 2 * p]
            state_sc[b, p, hh:, hh:] = s0_ref[b, 2 * p + 1]

    lane = lax.broadcasted_iota(jnp.int32, (chunk, LANES), 1)
    left = lane < hh
    zero = jnp.zeros((chunk, LANES), F32)
    vi = lax.broadcasted_iota(jnp.int32, (LANES, LANES), 0)
    ki = lax.broadcasted_iota(jnp.int32, (LANES, LANES), 1)
    same_head = (vi < hh) == (ki < hh)
    seg = seg_ref[...]

    ps = [_bdot_nt(jnp.concatenate([w_ref[b, :, ls], r_ref[b, :, ls]], axis=0), state_sc[b, p])
          for b, p, ls in insts]
    ys = []
    for (b, p, ls), wr_s in zip(insts, ps):
        u = wr_s[:chunk] + u0_ref[b, :, ls]
        u_bd = jnp.concatenate([jnp.where(left, u, zero), jnp.where(left, zero, u)], axis=0)
        ys.append(wr_s[chunk:] + _bdot(mrb_ref[b, :, ls], u_bd) + y0_ref[b, :, ls])
        uv = jnp.concatenate([u.astype(BF16), v_ref[b, :, ls]], axis=0)
        bk = jnp.concatenate([bh_ref[b, :, ls], kh_ref[b, :, ls]], axis=0)
        upd = _bdot_tn(uv, bk)
        state_sc[b, p] = state_sc[b, p] * elast_ref[b, 0:1, ls] + jnp.where(same_head, upd, 0.0)
    for (b, p, ls), y in zip(insts, ys):
        y_ref[b, :, ls] = _pair_norm_gate(y, seg, lnw_ref[:, ls], lnb_ref[:, ls], bonus_ref[b, :, ls],
                                          g_ref[b, :, ls]).astype(y_ref.dtype)

    @pl.when(c == pl.num_programs(0) - 1)
    def _():
        for b, p, _ in insts:
            sout_ref[b, 2 * p] = state_sc[b, p, :hh, :hh]
            sout_ref[b, 2 * p + 1] = state_sc[b, p, hh:, hh:]


def _chunk_scan(chunk_ops, elast, state0, lw, batch, seq):
    chunk = min(SCAN_CHUNK, seq)
    n_chunks = seq // chunk
    tok = pl.BlockSpec((batch, chunk, RWKV_WIDTH), lambda c: (0, c, 0))
    ops3 = [a.reshape(batch, seq, RWKV_WIDTH) for a in chunk_ops]
    y, s_new = pl.pallas_call(
        functools.partial(_chunk_scan_kernel, chunk=chunk),
        grid=(n_chunks,),
        in_specs=[tok] * 10 + [pl.BlockSpec((batch, SUBLANES, RWKV_WIDTH), lambda c: (0, c, 0)),
                               _const_spec(state0.shape), _const_spec(lw["lnx_w"].shape),
                               _const_spec(lw["lnx_b"].shape), _const_spec(lw["seg_pair"].shape)],
        out_specs=[tok, _const_spec(state0.shape)],
        out_shape=[jax.ShapeDtypeStruct((batch, seq, RWKV_WIDTH), BF16),
                   jax.ShapeDtypeStruct(state0.shape, F32)],
        scratch_shapes=[pltpu.VMEM((batch, RWKV_PAIRS, LANES, LANES), F32)],
        compiler_params=_params("arbitrary"),
        name="rwkv_chunk_scan",
    )(*ops3, elast.reshape(batch, n_chunks * SUBLANES, RWKV_WIDTH), state0, lw["lnx_w"], lw["lnx_b"],
      lw["seg_pair"])
    return y.reshape(batch * seq, RWKV_WIDTH), s_new


def _wkv_step_kernel(r_ref, lw_ref, k_ref, v_ref, an_ref, bn_ref, g_ref, bonus_ref, s_ref, lnw_ref, lnb_ref,
                     y_ref, sout_ref, yraw_sc):
    decay = jnp.exp(lw_ref[...])
    a, b, k, r = an_ref[...], bn_ref[...], k_ref[...], r_ref[...]

    def value_row(vi, carry):
        row = pl.ds(vi, 1)
        state = s_ref[vi]
        sa = jnp.sum(state * a, axis=0, keepdims=True)
        new = state * decay + sa * b + v_ref[row, :] * k
        sout_ref[vi] = new
        yraw_sc[row, :] = jnp.sum(new * r, axis=0, keepdims=True)
        return carry

    lax.fori_loop(0, RWKV_HEAD, value_row, 0, unroll=WKV_VALUE_UNROLL)
    y = yraw_sc[...]
    cen = y - jnp.mean(y, axis=0, keepdims=True)
    var = jnp.mean(cen * cen, axis=0, keepdims=True)
    y_ref[...] = (cen * lax.rsqrt(var + LNX_EPS) * lnw_ref[...] + lnb_ref[...] + bonus_ref[...]) * g_ref[...]


def _wkv_step(layer, prep_t, state_t, lw):
    n_seq = state_t.shape[-1]
    n = RWKV_HEAD
    chan = pl.BlockSpec((n, n_seq), lambda h: (h, 0))
    lnw_t = jnp.broadcast_to(lw["lnx_w"].reshape(RWKV_WIDTH, 1), (RWKV_WIDTH, n_seq))
    lnb_t = jnp.broadcast_to(lw["lnx_b"].reshape(RWKV_WIDTH, 1), (RWKV_WIDTH, n_seq))
    return pl.pallas_call(
        _wkv_step_kernel,
        grid=(RWKV_HEADS,),
        in_specs=[chan] * 8 + [pl.BlockSpec((None, None, n, n, n_seq), lambda h: (layer, h, 0, 0, 0)), chan, chan],
        out_specs=[chan, pl.BlockSpec((None, n, n, n_seq), lambda h: (h, 0, 0, 0))],
        out_shape=[jax.ShapeDtypeStruct((RWKV_WIDTH, n_seq), F32),
                   jax.ShapeDtypeStruct(state_t.shape[1:], F32)],
        scratch_shapes=[pltpu.VMEM((n, n_seq), F32)],
        compiler_params=_params("arbitrary"),
        name="wkv_step",
    )(*prep_t, state_t, lnw_t, lnb_t)


def _merge_kernel(x_ref, ym_ref, yr_ref, gate_ref, gt_ref, gpost_ref, wbm_ref, wbr_ref, wout_ref, o_ref):
    d = x_ref.shape[-1]
    merged = (gate_ref[:, :d] * _bdot(ym_ref[...], wbm_ref[...])
              + gate_ref[:, d:] * _bdot(yr_ref[...], wbr_ref[...]))
    out = _bdot(merged, wout_ref[...])
    o_ref[...] = x_ref[...] + gt_ref[...] * _rms(out, gpost_ref[...])


def _merge(x, y_mla, y_rw, gate, gt, g_post, lw, rows_per_group):
    t, d = x.shape
    tm = min(MERGE_ROW_TILE, rows_per_group)
    tpg = rows_per_group // tm
    row = lambda n: pl.BlockSpec((tm, n), lambda i: (i, 0))
    weights = [lw["w_branch_mla"], lw["w_branch_rwkv"], lw["w_out"]]
    return pl.pallas_call(
        _merge_kernel,
        grid=(t // tm,),
        in_specs=[row(d), row(MLA_WIDTH), row(RWKV_WIDTH), row(2 * d), _mod_spec(gt, tm, tpg),
                  _const_spec(g_post.shape)] + [_const_spec(w.shape) for w in weights],
        out_specs=row(d),
        out_shape=jax.ShapeDtypeStruct((t, d), F32),
        compiler_params=_params("arbitrary"),
        name="merge",
    )(x, y_mla, y_rw, gate, gt, g_post, *weights)


def _ffn_kernel(x_ref, sh_ref, sc_ref, gt_ref, gpre_ref, gpost_ref, w1_ref, w2_ref, o_ref):
    x = x_ref[...]
    hb = (_rms(x, gpre_ref[...]) * (1.0 + sc_ref[...]) + sh_ref[...]).astype(BF16)
    acc = jnp.zeros(x.shape, F32)
    for c0 in range(0, w1_ref.shape[1], FFN_CHUNK):
        a = jnp.maximum(jnp.dot(hb, w1_ref[:, c0:c0 + FFN_CHUNK], preferred_element_type=F32), 0.0)
        acc = acc + _bdot(a * a, w2_ref[c0:c0 + FFN_CHUNK, :])
    o_ref[...] = x + gt_ref[...] * _rms(acc, gpost_ref[...])


def _ffn(layer, x, sh, sc, gt, g_pre, g_post, w_ff1_all, w_ff2_all, rows_per_group):
    t, d = x.shape
    tm = min(FFN_ROW_TILE, rows_per_group)
    tpg = rows_per_group // tm
    row = pl.BlockSpec((tm, d), lambda i: (i, 0))
    of_layer = lambda w: pl.BlockSpec((None,) + w.shape[1:], lambda i: (layer, 0, 0))
    return pl.pallas_call(
        _ffn_kernel,
        grid=(t // tm,),
        in_specs=[row, _mod_spec(sh, tm, tpg), _mod_spec(sc, tm, tpg), _mod_spec(gt, tm, tpg),
                  _const_spec(g_pre.shape), _const_spec(g_post.shape),
                  of_layer(w_ff1_all), of_layer(w_ff2_all)],
        out_specs=row,
        out_shape=jax.ShapeDtypeStruct((t, d), F32),
        compiler_params=_params("arbitrary"),
        name="ffn",
    )(x, sh, sc, gt, g_pre, g_post, w_ff1_all, w_ff2_all)


def _rope_tables(pos):
    inv = ROPE_BASE ** (-jnp.arange(HALF_ROPE, dtype=F32) / HALF_ROPE)
    ang = pos.astype(F32)[:, None] * inv[None, :]
    cos, sin = jnp.cos(ang), jnp.sin(ang)
    n = pos.shape[0]
    ones = jnp.ones((n, ROPE_LANE0), F32)
    zeros = jnp.zeros((n, ROPE_LANE0), F32)
    pad = jnp.zeros((n, LANES - ROPE_LANE0 - QK_ROPE), F32)
    z16 = jnp.zeros((n, HALF_ROPE), F32)
    cos_t = jnp.concatenate([ones, cos, cos, pad], axis=1)
    sin_lo = jnp.concatenate([zeros, -sin, z16, pad], axis=1)
    sin_hi = jnp.concatenate([zeros, z16, sin, pad], axis=1)
    return cos_t, sin_lo, sin_hi


def _layer_weights(l, p):
    w_in = p["w_in"][l]
    w_q = w_in[:, :Q_RANK]
    w_ckv = w_in[:, Q_RANK:Q_RANK + KV_RANK]
    w_kpe = jnp.pad(w_in[:, Q_RANK + KV_RANK:MLA_PROJ], ((0, 0), (ROPE_LANE0, LANES - ROPE_LANE0 - QK_ROPE)))
    w_rw = w_in[:, MLA_PROJ:MLA_PROJ + RWKV_PROJ]
    w_gate = w_in[:, MLA_PROJ + RWKV_PROJ:]
    qk = QK_NOPE + QK_ROPE
    w_uq = jnp.pad(p["w_uq"][l].reshape(Q_RANK, MLA_HEADS, qk), ((0, 0), (0, 0), (0, HEAD_SLAB - qk)))
    w_ukv = p["w_ukv"][l].reshape(KV_RANK, MLA_HEADS, QK_NOPE + V_HEAD)
    w_uk, w_uv = w_ukv[..., :QK_NOPE], w_ukv[..., QK_NOPE:]
    w_ukn = jnp.pad(w_uk, ((0, 0), (0, 0), (0, HEAD_SLAB - QK_NOPE)))
    w_uk_t = jnp.pad(w_uk.transpose(1, 2, 0), ((0, 0), (0, HEAD_SLAB - QK_NOPE), (0, 0)))
    eye = jnp.eye(MLA_HEADS, dtype=F32)
    w_uv_wide = (w_uv.transpose(1, 0, 2)[:, :, None, :] * eye[:, None, :, None]).reshape(
        MLA_HEADS, KV_RANK, MLA_WIDTH)
    lora_pad_lo = ((0, AAA_LORA), (0, 0))
    lora_pad_hi = ((DECAY_LORA, 0), (0, 0))
    heads = jnp.arange(RWKV_WIDTH) // RWKV_HEAD
    seg_ones = (heads[:, None] == heads[None, :]).astype(BF16)
    row2 = lambda a: a.reshape(1, -1)
    return {
        "w_q": w_q.astype(BF16), "w_ckv": w_ckv.astype(BF16), "w_kpe": w_kpe.astype(BF16),
        "w_rw": w_rw.astype(BF16), "w_gate": w_gate.astype(BF16),
        "g_q": row2(p["g_q_norm"][l]), "g_kv": row2(p["g_kv_norm"][l]),
        "w_uq": w_uq.reshape(Q_RANK, MLA_HEADS * HEAD_SLAB).astype(BF16),
        "w_ukn": w_ukn.reshape(KV_RANK, MLA_HEADS * HEAD_SLAB).astype(BF16),
        "w_uv_t": w_uv.reshape(KV_RANK, MLA_WIDTH).T.astype(BF16),
        "w_uk_t": w_uk_t.astype(BF16), "w_uv_wide": w_uv_wide.astype(BF16),
        "mu": row2(p["mu_shift"][l]),
        "rw_w0": row2(p["rw_w0"][l]), "rw_a0": row2(p["rw_a0"][l]),
        "rw_w2": jnp.pad(p["rw_w2"][l], lora_pad_lo).astype(BF16),
        "rw_a2": jnp.pad(p["rw_a2"][l], lora_pad_hi).astype(BF16),
        "rw_g2": p["rw_g2"][l].astype(BF16),
        "rw_k_k": row2(p["rw_k_k"][l]), "rw_k_a": row2(p["rw_k_a"][l]), "rw_r_k": row2(p["rw_r_k"][l]),
        "seg_ones": seg_ones, "seg_pair": seg_ones[:LANES, :LANES],
        "lnx_w": row2(p["rw_lnx_w"][l]), "lnx_b": row2(p["rw_lnx_b"][l]),
        "w_branch_mla": p["w_branch_mla"][l].astype(BF16), "w_branch_rwkv": p["w_branch_rwkv"][l].astype(BF16),
        "w_out": p["w_out"][l].astype(BF16),
        "g_pre_mix": row2(p["g_pre_mix"][l]), "g_post_mix": row2(p["g_post_mix"][l]),
        "g_pre_ff": row2(p["g_pre_ff"][l]), "g_post_ff": row2(p["g_post_ff"][l]),
    }


def _trunk(x, mods, rope_tabs, rows_per_group, layer_weights, w_ff1_all, w_ff2_all, mixers):
    ckv_rows, kpe_rows, wkv_states, shift_rows = [], [], [], []
    for l in range(DEPTH):
        lw = layer_weights[l]
        sh1, sc1, gt1, sh2, sc2, gt2 = mods[l]
        q, k, vt, ckv, kpe, z_rw, gate = _inproj(x, sh1, sc1, lw["g_pre_mix"], rope_tabs, lw, rows_per_group)
        y_mla, y_rw, wkv_new, shift_new = mixers(l, lw, q, k, vt, ckv, kpe, z_rw)
        x = _merge(x, y_mla, y_rw, gate, gt1, lw["g_post_mix"], lw, rows_per_group)
        x = _ffn(l, x, sh2, sc2, gt2, lw["g_pre_ff"], lw["g_post_ff"], w_ff1_all, w_ff2_all, rows_per_group)
        ckv_rows.append(ckv)
        kpe_rows.append(kpe)
        wkv_states.append(wkv_new)
        shift_rows.append(shift_new)
    return x, jnp.stack(ckv_rows), jnp.stack(kpe_rows), jnp.stack(wkv_states), jnp.stack(shift_rows)


def kernel(x_prompt, x_sample, c_prompt, c_sample, cache_ckv, cache_kpe, state_wkv, state_shift, page_table, w_ada, b_ada, g_pre_mix, g_post_mix, g_pre_ff, g_post_ff, w_in, mu_shift, g_q_norm, w_uq, g_kv_norm, w_ukv, rw_w0, rw_w2, rw_a0, rw_a2, rw_g2, rw_k_k, rw_k_a, rw_r_k, rw_lnx_w, rw_lnx_b, w_branch_mla, w_branch_rwkv, w_out, w_ff1, w_ff2):
    params = dict(g_pre_mix=g_pre_mix, g_post_mix=g_post_mix, g_pre_ff=g_pre_ff, g_post_ff=g_post_ff,
                  w_in=w_in, mu_shift=mu_shift, g_q_norm=g_q_norm, w_uq=w_uq, g_kv_norm=g_kv_norm,
                  w_ukv=w_ukv, rw_w0=rw_w0, rw_w2=rw_w2, rw_a0=rw_a0, rw_a2=rw_a2, rw_g2=rw_g2,
                  rw_k_k=rw_k_k, rw_k_a=rw_k_a, rw_r_k=rw_r_k, rw_lnx_w=rw_lnx_w, rw_lnx_b=rw_lnx_b,
                  w_branch_mla=w_branch_mla, w_branch_rwkv=w_branch_rwkv, w_out=w_out,
                  w_ff1=w_ff1, w_ff2=w_ff2)
    b_p, s_p, d = x_prompt.shape
    b_s, s_s, _ = x_sample.shape
    assert s_s == 1, "the sample group advances one token per sequence"
    past_len = page_table.shape[1] * cache_ckv.shape[2]
    layer_weights = [_layer_weights(l, params) for l in range(DEPTH)]
    w_ff1_all, w_ff2_all = w_ff1.astype(BF16), w_ff2.astype(BF16)

    mod = _adaln(jnp.concatenate([c_prompt, c_sample], axis=0), w_ada, b_ada)
    mod = mod.reshape(DEPTH, b_p + b_s, 6, d)
    mods_p = [[mod[l, :b_p, j].reshape(b_p, 1, d) for j in range(6)] for l in range(DEPTH)]
    mods_s = [[mod[l, b_p:, j].reshape(1, b_s, d) for j in range(6)] for l in range(DEPTH)]

    shift0_p = jnp.zeros((b_p, RWKV_PROJ), F32)
    wkv0_p = jnp.zeros((b_p, RWKV_HEADS, RWKV_HEAD, RWKV_HEAD), F32)

    def prompt_mixers(l, lw, q, k, vt, ckv, kpe, z_rw):
        y_mla = _prompt_attention(q, k, vt, b_p, s_p)
        chunk_ops, elast, last = _chunk_prep(z_rw, shift0_p, lw, b_p, s_p)
        y_rw, wkv_new = _chunk_scan(chunk_ops, elast, wkv0_p, lw, b_p, s_p)
        return y_mla, y_rw, wkv_new, last.reshape(b_p, RWKV_PROJ)

    y_p, p_ckv, p_kpe, p_wkv, p_shift = _trunk(
        x_prompt.reshape(b_p * s_p, d), mods_p, _rope_tables(jnp.arange(s_p)), s_p, layer_weights,
        w_ff1_all, w_ff2_all, prompt_mixers)

    cache_kpe_t = cache_kpe.transpose(0, 1, 3, 2)
    state_wkv_t = state_wkv.transpose(0, 2, 3, 4, 1)

    def sample_mixers(l, lw, q, k, vt, ckv, kpe, z_rw):
        y_mla = _sample_attention(l, q, ckv, kpe, cache_ckv, cache_kpe_t, page_table, lw)
        prep_t = _token_prep(z_rw, state_shift[l], lw)
        y_rw_t, wkv_new_t = _wkv_step(l, prep_t, state_wkv_t, lw)
        return y_mla, y_rw_t.T, wkv_new_t, z_rw

    y_s, s_ckv, s_kpe, s_wkv_t, s_shift = _trunk(
        x_sample.reshape(b_s, d), mods_s, _rope_tables(jnp.full((b_s,), past_len)), b_s, layer_weights,
        w_ff1_all, w_ff2_all, sample_mixers)
    s_wkv = s_wkv_t.transpose(0, 4, 1, 2, 3)

    return (y_p.reshape(b_p, s_p, d), y_s.reshape(b_s, 1, d),
            p_ckv.reshape(DEPTH, b_p, s_p, KV_RANK), p_kpe.reshape(DEPTH, b_p, s_p, QK_ROPE), p_wkv, p_shift,
            s_ckv.reshape(DEPTH, b_s, 1, KV_RANK), s_kpe.reshape(DEPTH, b_s, 1, QK_ROPE), s_wkv, s_shift)
```
